```python
import math
import jax, jax.numpy as jnp
from jax import lax
import numpy as np

D_MODEL = 1024
BATCH = 16
SEQ = 2048
DEPTH = 1
DEC_BATCH = 32
DEC_SEQ = 2048
PAST_LEN = 128

HEAD_DIM = 64
DIL_PATTERNS = ((128, 1), (512, 4), (2048, 16))
N_GROUPS_A = 3
HEADS_PER_GROUP_A = 6
N_HEADS_A = N_GROUPS_A * HEADS_PER_GROUP_A
WIDTH_A = N_HEADS_A * HEAD_DIM
DA_QBLOCK = 64
N_HEADS_B = 14
WIDTH_B = N_HEADS_B * HEAD_DIM
GRID_W = 64
NA_KH = 8
NA_KW = 16
NA_QCB = 16
IN_COLS = 4 * WIDTH_A + 4 * WIDTH_B + 2 * D_MODEL
SPLIT_POINTS = (WIDTH_A, 2 * WIDTH_A, 3 * WIDTH_A, 4 * WIDTH_A,
                4 * WIDTH_A + WIDTH_B, 4 * WIDTH_A + 2 * WIDTH_B, 4 * WIDTH_A + 3 * WIDTH_B, 4 * WIDTH_A + 4 * WIDTH_B,
                4 * WIDTH_A + 4 * WIDTH_B + D_MODEL)
RMS_EPS = 1e-6
NEG_INF = -1e30

kernel_name = "hybrid_dilated_neighbourhood_encoder"


def rms_norm(x, g):
    xf = x.astype(jnp.float32)
    y = xf * lax.rsqrt(jnp.mean(xf * xf, axis=-1, keepdims=True) + RMS_EPS)
    return (y * g.astype(jnp.float32)).astype(x.dtype)


def alibi_slopes():
    return jnp.asarray((2.0 ** (-8.0 * np.arange(1, N_HEADS_A + 1) / N_HEADS_A)).astype(np.float32))


def dilated_window_attention(q, k, v, window, dilation, slopes):
    b, t, h, e = q.shape
    d = dilation
    L = t // d
    half = (window // 2) // d
    qblk = math.gcd(L, DA_QBLOCK)
    nb = L // qblk
    kl = qblk + 2 * half
    qs = q.reshape(b, nb, qblk, d, h, e)
    pad = ((0, 0), (half, half), (0, 0), (0, 0), (0, 0))
    kp = jnp.pad(k.reshape(b, L, d, h, e), pad)
    vp = jnp.pad(v.reshape(b, L, d, h, e), pad)
    kidx = np.arange(nb)[:, None] * qblk + np.arange(kl)[None, :]
    kb = kp[:, kidx]
    vb = vp[:, kidx]
    s = jnp.einsum('bnqrhe,bnkrhe->bnrhqk', qs, kb, preferred_element_type=jnp.float32) * (e ** -0.5)
    off = np.arange(kl)[None, :] - half - np.arange(qblk)[:, None]
    kpos = kidx - half
    valid = (np.abs(off) <= half)[None] & ((kpos >= 0) & (kpos < L))[:, None, :]
    dist = (np.abs(off) * d).astype(np.float32)
    s = s - slopes[:, None, None] * dist
    s = jnp.where(valid[None, :, None, None], s, NEG_INF)
    lse = jax.nn.logsumexp(s, axis=-1)
    p = jnp.exp(s - lse[..., None])
    o = jnp.einsum('bnrhqk,bnkrhe->bnqrhe', p.astype(v.dtype), vb).reshape(b, t, h, e)
    lse = lse.transpose(0, 1, 4, 2, 3).reshape(b, t, h)
    return o, lse


def neighbourhood_attention(q, k, v, rpb):
    b, t, h, e = q.shape
    rows = t // GRID_W
    kh = min(NA_KH, rows)
    kw = NA_KW
    slab = NA_QCB + kw
    ncb = GRID_W // NA_QCB
    r = np.arange(rows)
    rs = np.clip(r - kh // 2, 0, rows - kh)
    key_rows = rs[:, None] + np.arange(kh)[None, :]
    c0 = np.arange(ncb) * NA_QCB
    s0 = np.clip(c0 - kw // 2, 0, GRID_W - slab)
    key_cols = s0[:, None] + np.arange(slab)[None, :]
    nk = kh * slab
    idx = (key_rows[:, None, :, None] * GRID_W + key_cols[None, :, None, :]).reshape(rows, ncb, nk)
    qcol = c0[:, None] + np.arange(NA_QCB)[None, :]
    cs = np.clip(qcol - kw // 2, 0, GRID_W - kw)
    col_valid = (key_cols[:, None, :] >= cs[:, :, None]) & (key_cols[:, None, :] < cs[:, :, None] + kw)
    valid = np.broadcast_to(col_valid[:, :, None, :], (ncb, NA_QCB, kh, slab)).reshape(ncb, NA_QCB, nk)
    dr = key_rows - r[:, None] + NA_KH - 1
    dc = np.clip(key_cols[:, None, :] - qcol[:, :, None], -(kw - 1), kw - 1) + kw - 1
    bias = rpb[:, dr[:, None, None, :, None], dc[None, :, :, None, :]]
    bias = bias.astype(jnp.float32).reshape(h, rows, ncb, NA_QCB, nk).transpose(1, 2, 0, 3, 4)
    qb = q.reshape(b, rows, ncb, NA_QCB, h, e)
    kb = k[:, idx]
    vb = v[:, idx]
    s = jnp.einsum('brcqhe,brckhe->brchqk', qb, kb, preferred_element_type=jnp.float32) * (e ** -0.5)
    s = s + bias[None]
    s = jnp.where(valid[None, None, :, None, :, :], s, NEG_INF)
    p = jax.nn.softmax(s, axis=-1)
    return jnp.einsum('brchqk,brckhe->brcqhe', p.astype(v.dtype), vb).reshape(b, t, h, e)


def encoder_layer(x, norm_pre, w_in, b_gate, rpb, w_proj_a, w_proj_b, w_out, norm_post):
    b, t, _ = x.shape
    hn = rms_norm(x, norm_pre)
    proj = hn @ w_in
    qa, ka, va, za, qb, kb, vb, zb, ga, gb = jnp.split(proj, SPLIT_POINTS, axis=-1)
    qa = qa.reshape(b, t, N_HEADS_A, HEAD_DIM)
    ka = ka.reshape(b, t, N_HEADS_A, HEAD_DIM)
    va = va.reshape(b, t, N_HEADS_A, HEAD_DIM)
    slopes = alibi_slopes()
    outs, lses = [], []
    for g, (window, dilation) in enumerate(DIL_PATTERNS):
        sl = slice(g * HEADS_PER_GROUP_A, (g + 1) * HEADS_PER_GROUP_A)
        o, lse = dilated_window_attention(qa[:, :, sl], ka[:, :, sl], va[:, :, sl], window, dilation,
                                          slopes[g::N_GROUPS_A])
        outs.append(o)
        lses.append(lse)
    alpha = jax.nn.softmax(jnp.stack(lses, axis=0), axis=0)
    oa = jnp.concatenate([o * alpha[g][..., None].astype(o.dtype) for g, o in enumerate(outs)], axis=2)
    ya = (oa.reshape(b, t, WIDTH_A) * jax.nn.silu(za)) @ w_proj_a
    ob = neighbourhood_attention(qb.reshape(b, t, N_HEADS_B, HEAD_DIM), kb.reshape(b, t, N_HEADS_B, HEAD_DIM),
                                 vb.reshape(b, t, N_HEADS_B, HEAD_DIM), rpb)
    yb = (ob.reshape(b, t, WIDTH_B) * jax.nn.silu(zb)) @ w_proj_b
    merged = jax.nn.sigmoid(ga + b_gate[0]) * ya + jax.nn.sigmoid(gb + b_gate[1]) * yb
    out = merged @ w_out
    return x + rms_norm(out, norm_post)


def setup_inputs(seed: int = 0) -> dict:
    key = jax.random.key(seed)
    ks = jax.random.split(key, 10)
    f32 = jnp.float32
    return {
        "x_prompt": jax.random.normal(ks[0], (BATCH, SEQ, D_MODEL), f32),
        "x_sample": jax.random.normal(ks[1], (DEC_BATCH, DEC_SEQ, D_MODEL), f32),
        "norm_pre": 1.0 + 0.1 * jax.random.normal(ks[2], (DEPTH, D_MODEL), f32),
        "w_in": jax.random.normal(ks[3], (DEPTH, D_MODEL, IN_COLS), f32) * D_MODEL ** -0.5,
        "b_gate": 0.1 * jax.random.normal(ks[4], (DEPTH, 2, D_MODEL), f32),
        "rpb": 0.1 * jax.random.normal(ks[5], (DEPTH, N_HEADS_B, 2 * NA_KH - 1, 2 * NA_KW - 1), f32),
        "w_proj_a": jax.random.normal(ks[6], (DEPTH, WIDTH_A, D_MODEL), f32) * WIDTH_A ** -0.5,
        "w_proj_b": jax.random.normal(ks[7], (DEPTH, WIDTH_B, D_MODEL), f32) * WIDTH_B ** -0.5,
        "w_out": jax.random.normal(ks[8], (DEPTH, D_MODEL, D_MODEL), f32) * D_MODEL ** -0.5,
        "norm_post": 1.0 + 0.1 * jax.random.normal(ks[9], (DEPTH, D_MODEL), f32),
    }


def reference(x_prompt, x_sample, norm_pre, w_in, b_gate, rpb, w_proj_a, w_proj_b, w_out, norm_post):
    y_prompt = x_prompt
    y_sample = x_sample
    for l in range(DEPTH):
        y_prompt = encoder_layer(y_prompt, norm_pre[l], w_in[l], b_gate[l], rpb[l], w_proj_a[l], w_proj_b[l],
                                 w_out[l], norm_post[l])
        y_sample = encoder_layer(y_sample, norm_pre[l], w_in[l], b_gate[l], rpb[l], w_proj_a[l], w_proj_b[l],
                                 w_out[l], norm_post[l])
    return (y_prompt, y_sample)
```

```python
import functools
import math

import numpy as np
import jax
import jax.numpy as jnp
from jax import lax
from jax.experimental import pallas as pl
from jax.experimental.pallas import tpu as pltpu

D_MODEL = 1024
SEQ = 2048
HEAD_DIM = 64
DIL_PATTERNS = ((128, 1), (512, 4), (2048, 16))
N_GROUPS_A = 3
HEADS_PER_GROUP_A = 6
N_HEADS_A = N_GROUPS_A * HEADS_PER_GROUP_A
WIDTH_A = N_HEADS_A * HEAD_DIM
GROUP_W = HEADS_PER_GROUP_A * HEAD_DIM
N_HEADS_B = 14
WIDTH_B = N_HEADS_B * HEAD_DIM
GRID_W = 64
GRID_ROWS = SEQ // GRID_W
NA_KH = 8
NA_KW = 16
RMS_EPS = 1e-6
NEG_INF = -1e30

LANES = 128
PAIR_W = 2 * HEAD_DIM
PAIRS_A = HEADS_PER_GROUP_A // 2
PAIRS_B = N_HEADS_B // 2
HALF_WIN = 64
QBLK_A = 128
KWIN_A = QBLK_A + 2 * HALF_WIN
NBR_KEYS = NA_KH * GRID_W
VMEM_LIMIT = 56 * 1024 * 1024

QKV_A_W = 3 * GROUP_W
QKV_B_W = 3 * WIDTH_B
GATE_W = 2 * D_MODEL
OUT_WIDTHS = (QKV_A_W, QKV_A_W, QKV_A_W, WIDTH_A, QKV_B_W, WIDTH_B, GATE_W)
IN_COLS = sum(OUT_WIDTHS)
PROJ_CHUNK = 512
TM_IN = 512
TM_OUT = 512


def _permute_w_in(w_in):
    scale = HEAD_DIM ** -0.5
    qa, ka, va, za = (w_in[:, i * WIDTH_A:(i + 1) * WIDTH_A] for i in range(4))
    ob = 4 * WIDTH_A
    qb, kb, vb, zb = (w_in[:, ob + i * WIDTH_B: ob + (i + 1) * WIDTH_B] for i in range(4))
    gates = w_in[:, ob + 4 * WIDTH_B:]
    cols = []
    for g in range(N_GROUPS_A):
        sl = slice(g * GROUP_W, (g + 1) * GROUP_W)
        cols += [qa[:, sl] * scale, ka[:, sl], va[:, sl]]
    cols += [za, qb * scale, kb, vb, zb, gates]
    return jnp.concatenate(cols, axis=1).astype(jnp.bfloat16)


def _inproj_kernel(x_ref, g_ref, w_ref, *out_refs):
    x = x_ref[0]
    ms = jnp.mean(x * x, axis=-1, keepdims=True)
    hn = (x * lax.rsqrt(ms + RMS_EPS) * g_ref[...]).astype(jnp.bfloat16)
    bounds = np.cumsum((0,) + OUT_WIDTHS)
    for c0 in range(0, IN_COLS, PROJ_CHUNK):
        c1 = c0 + PROJ_CHUNK
        res = jnp.dot(hn, w_ref[:, c0:c1], preferred_element_type=jnp.float32).astype(jnp.bfloat16)
        for oi, o_ref in enumerate(out_refs):
            lo, hi = max(c0, int(bounds[oi])), min(c1, int(bounds[oi + 1]))
            if lo < hi:
                o_ref[0, :, lo - int(bounds[oi]):hi - int(bounds[oi])] = res[:, lo - c0:hi - c0]


def _inproj_call(x, norm_pre, w_perm):
    b, t, _ = x.shape
    out_shape = [jax.ShapeDtypeStruct((b, t, w), jnp.bfloat16) for w in OUT_WIDTHS]
    out_specs = [pl.BlockSpec((1, TM_IN, w), lambda i, j: (i, j, 0)) for w in OUT_WIDTHS]
    return pl.pallas_call(
        _inproj_kernel,
        grid=(b, t // TM_IN),
        in_specs=[
            pl.BlockSpec((1, TM_IN, D_MODEL), lambda i, j: (i, j, 0)),
            pl.BlockSpec((1, D_MODEL), lambda i, j: (0, 0)),
            pl.BlockSpec((D_MODEL, IN_COLS), lambda i, j: (0, 0), pipeline_mode=pl.Buffered(1)),
        ],
        out_specs=out_specs,
        out_shape=out_shape,
        compiler_params=pltpu.CompilerParams(
            dimension_semantics=("arbitrary", "arbitrary"), vmem_limit_bytes=VMEM_LIMIT),
        name="inproj",
    )(x, norm_pre.reshape(1, D_MODEL), w_perm)


def _alibi_slopes():
    return (2.0 ** (-8.0 * np.arange(1, N_HEADS_A + 1) / N_HEADS_A)).astype(np.float32)


def _dilated_bias_table(g, seq_len):
    _, d = DIL_PATTERNS[g]
    sub_len = seq_len // d
    kwin = min(KWIN_A, sub_len)
    deltas = (-HALF_WIN, 0, sub_len - kwin - (sub_len - QBLK_A)) if sub_len > QBLK_A else (0,)
    slopes = _alibi_slopes()[g::N_GROUPS_A]
    i = np.arange(QBLK_A)[:, None]
    j = np.arange(kwin)[None, :]
    tab = np.empty((PAIRS_A, len(deltas), 2 * QBLK_A, kwin), np.float32)
    for vi, delta in enumerate(deltas):
        rel = np.abs(j - i + delta)
        for p in range(PAIRS_A):
            for hh in range(2):
                bias = np.where(rel <= HALF_WIN, -slopes[2 * p + hh] * (rel * d).astype(np.float32), NEG_INF)
                tab[p, vi, hh * QBLK_A:(hh + 1) * QBLK_A] = bias
    return jnp.asarray(tab)


def _pair_softmax_pv(qs, k2, v2, bias):
    s = lax.dot_general(qs, k2, (((1,), (1,)), ((), ())), preferred_element_type=jnp.float32) + bias
    m = jnp.max(s, axis=-1, keepdims=True)
    e = jnp.exp(s - m)
    l = jnp.sum(e, axis=-1, keepdims=True)
    pv = jnp.dot(e.astype(jnp.bfloat16), v2, preferred_element_type=jnp.float32)
    return pv, m, l


def _stack_heads(q2):
    lane = lax.broadcasted_iota(jnp.int32, q2.shape, 1)
    zero = jnp.zeros_like(q2)
    return jnp.concatenate([jnp.where(lane < HEAD_DIM, q2, zero), jnp.where(lane < HEAD_DIM, zero, q2)], axis=0)


def _dilated_kernel(qkv_ref, bias_ref, o_ref, lse_ref, *, sub_len, n_res):
    kwin = min(KWIN_A, sub_len)
    n_blk = sub_len // QBLK_A
    lane = lax.broadcasted_iota(jnp.int32, (QBLK_A, LANES), 1)

    def block(n, res):
        base = res * QKV_A_W
        if n_blk == 1:
            q0, k0, var = 0, 0, 0
        else:
            q0 = pl.multiple_of(n * QBLK_A, QBLK_A)
            k0 = pl.multiple_of(jnp.clip(n * QBLK_A - HALF_WIN, 0, sub_len - kwin), HALF_WIN)
            var = jnp.where(n == 0, 1, jnp.where(n == n_blk - 1, 2, 0))
        lse_tile = jnp.zeros((QBLK_A, LANES), jnp.float32)
        for p in range(PAIRS_A):
            c = base + p * PAIR_W
            q2 = qkv_ref[0, pl.ds(q0, QBLK_A), c:c + PAIR_W]
            k2 = qkv_ref[0, pl.ds(k0, kwin), c + GROUP_W:c + GROUP_W + PAIR_W]
            v2 = qkv_ref[0, pl.ds(k0, kwin), c + 2 * GROUP_W:c + 2 * GROUP_W + PAIR_W]
            pv, m, l = _pair_softmax_pv(_stack_heads(q2), k2, v2, bias_ref[p, var])
            pvn = pv * (1.0 / l)
            o2 = jnp.where(lane < HEAD_DIM, pvn[:QBLK_A], pvn[QBLK_A:])
            o_ref[0, pl.ds(q0, QBLK_A), res * GROUP_W + p * PAIR_W:res * GROUP_W + (p + 1) * PAIR_W] = (
                o2.astype(o_ref.dtype))
            lse = m + jnp.log(l)
            lse_tile = jnp.where(lane == 2 * p, lse[:QBLK_A], lse_tile)
            lse_tile = jnp.where(lane == 2 * p + 1, lse[QBLK_A:], lse_tile)
        lse_ref[0, pl.ds(q0, QBLK_A), res * LANES:(res + 1) * LANES] = lse_tile

    for res in range(n_res):
        if n_blk == 1:
            block(0, res)
        else:
            def body(n, carry, res=res):
                block(n, res)
                return carry
            lax.fori_loop(0, n_blk, body, 0)


def _dilated_call(qkv, g):
    b, t, _ = qkv.shape
    _, d = DIL_PATTERNS[g]
    sub_len = t // d
    n_res = min(d, 4)
    bias = _dilated_bias_table(g, t)
    qkv_v = qkv.reshape(b, sub_len, d * QKV_A_W)
    o, lse = pl.pallas_call(
        functools.partial(_dilated_kernel, sub_len=sub_len, n_res=n_res),
        grid=(b, d // n_res),
        in_specs=[
            pl.BlockSpec((1, sub_len, n_res * QKV_A_W), lambda i, j: (i, 0, j)),
            pl.BlockSpec(bias.shape, lambda i, j: (0, 0, 0, 0)),
        ],
        out_specs=[
            pl.BlockSpec((1, sub_len, n_res * GROUP_W), lambda i, j: (i, 0, j)),
            pl.BlockSpec((1, sub_len, n_res * LANES), lambda i, j: (i, 0, j)),
        ],
        out_shape=[
            jax.ShapeDtypeStruct((b, sub_len, d * GROUP_W), jnp.bfloat16),
            jax.ShapeDtypeStruct((b, sub_len, d * LANES), jnp.float32),
        ],
        compiler_params=pltpu.CompilerParams(
            dimension_semantics=("arbitrary", "arbitrary"), vmem_limit_bytes=VMEM_LIMIT),
        name=f"dilated_g{g}",
    )(qkv_v, bias)
    return o.reshape(b, t, GROUP_W), lse.reshape(b, t, LANES)


def _nbr_bias_table(rpb):
    qc = np.arange(GRID_W)[:, None]
    kc = np.arange(GRID_W)[None, :]
    cs = np.clip(qc - NA_KW // 2, 0, GRID_W - NA_KW)
    valid = (kc >= cs) & (kc < cs + NA_KW)
    dc = np.clip(kc - qc, -(NA_KW - 1), NA_KW - 1) + NA_KW - 1
    dr = np.arange(NA_KH)[:, None] + np.arange(NA_KH)[None, :]
    tab = rpb.astype(jnp.float32)[:, dr[:, :, None, None], dc[None, None]]
    tab = jnp.where(valid[None, None, None], tab, NEG_INF)
    tab = tab.transpose(0, 1, 3, 2, 4).reshape(N_HEADS_B, NA_KH, GRID_W, NBR_KEYS)
    return tab.reshape(PAIRS_B, 2, NA_KH, GRID_W, NBR_KEYS).transpose(0, 2, 1, 3, 4).reshape(
        PAIRS_B, NA_KH, 2 * GRID_W, NBR_KEYS)


def _nbr_kernel(q_ref, k_ref, v_ref, bias_ref, o_ref):
    lane = lax.broadcasted_iota(jnp.int32, (GRID_W, LANES), 1)

    def body(r, carry):
        rs = jnp.clip(r - NA_KH // 2, 0, GRID_ROWS - NA_KH)
        var = rs - r + NA_KH - 1
        q0 = pl.multiple_of(r * GRID_W, GRID_W)
        k0 = pl.multiple_of(rs * GRID_W, GRID_W)
        q2 = q_ref[0, pl.ds(q0, GRID_W), :]
        k2 = k_ref[0, pl.ds(k0, NBR_KEYS), :]
        v2 = v_ref[0, pl.ds(k0, NBR_KEYS), :]
        pv, _, l = _pair_softmax_pv(_stack_heads(q2), k2, v2, bias_ref[0, var])
        pvn = pv * (1.0 / l)
        o_ref[0, pl.ds(q0, GRID_W), :] = jnp.where(lane < HEAD_DIM, pvn[:GRID_W], pvn[GRID_W:]).astype(o_ref.dtype)
        return carry

    lax.fori_loop(0, GRID_ROWS, body, 0)


def _nbr_call(qkv, bias):
    b, t, _ = qkv.shape
    return pl.pallas_call(
        _nbr_kernel,
        grid=(b, PAIRS_B),
        in_specs=[
            pl.BlockSpec((1, t, PAIR_W), lambda i, p: (i, 0, p)),
            pl.BlockSpec((1, t, PAIR_W), lambda i, p: (i, 0, PAIRS_B + p)),
            pl.BlockSpec((1, t, PAIR_W), lambda i, p: (i, 0, 2 * PAIRS_B + p)),
            pl.BlockSpec((1, NA_KH, 2 * GRID_W, NBR_KEYS), lambda i, p: (p, 0, 0, 0)),
        ],
        out_specs=pl.BlockSpec((1, t, PAIR_W), lambda i, p: (i, 0, p)),
        out_shape=jax.ShapeDtypeStruct((b, t, WIDTH_B), jnp.bfloat16),
        compiler_params=pltpu.CompilerParams(
            dimension_semantics=("arbitrary", "arbitrary"), vmem_limit_bytes=VMEM_LIMIT),
        name="nbr",
    )(qkv, qkv, qkv, bias)


def _silu(z):
    return z * jax.nn.sigmoid(z)


def _merge_kernel(x_ref, oa0_ref, oa1_ref, oa2_ref, l0_ref, l1_ref, l2_ref, za_ref, ob_ref, zb_ref, g_ref,
                  expand_ref, wpa_ref, wpb_ref, wout_ref, bg_ref, npost_ref, y_ref):
    f32, bf16 = jnp.float32, jnp.bfloat16
    lses = [r[0] for r in (l0_ref, l1_ref, l2_ref)]
    mx = jnp.maximum(jnp.maximum(lses[0], lses[1]), lses[2])
    es = [jnp.exp(l - mx) for l in lses]
    inv = 1.0 / (es[0] + es[1] + es[2])
    ua = []
    for g, (e, oa_ref) in enumerate(zip(es, (oa0_ref, oa1_ref, oa2_ref))):
        alpha = e * inv
        hi = alpha.astype(bf16)
        lo = (alpha - hi.astype(f32)).astype(bf16)
        aexp = (jnp.dot(hi, expand_ref[...], preferred_element_type=f32)
                + jnp.dot(lo, expand_ref[...], preferred_element_type=f32))
        za = za_ref[0, :, g * GROUP_W:(g + 1) * GROUP_W].astype(f32)
        ua.append((oa_ref[0].astype(f32) * aexp * _silu(za)).astype(bf16))
    ya = jnp.dot(jnp.concatenate(ua, axis=1), wpa_ref[...], preferred_element_type=f32)
    ub = (ob_ref[0].astype(f32) * _silu(zb_ref[0].astype(f32))).astype(bf16)
    yb = jnp.dot(ub, wpb_ref[...], preferred_element_type=f32)
    ga = g_ref[0, :, :D_MODEL].astype(f32) + bg_ref[0:1, :]
    gb = g_ref[0, :, D_MODEL:].astype(f32) + bg_ref[1:2, :]
    merged = jax.nn.sigmoid(ga) * ya + jax.nn.sigmoid(gb) * yb
    out = jnp.dot(merged.astype(bf16), wout_ref[...], preferred_element_type=f32)
    ms = jnp.mean(out * out, axis=-1, keepdims=True)
    y_ref[0] = x_ref[0] + out * lax.rsqrt(ms + RMS_EPS) * npost_ref[...]


def _merge_call(x, oas, lses, za, ob, zb, gates, b_gate, w_proj_a, w_proj_b, w_out, norm_post):
    b, t, _ = x.shape
    expand = np.zeros((LANES, GROUP_W), np.float32)
    for i in range(HEADS_PER_GROUP_A):
        expand[i, i * HEAD_DIM:(i + 1) * HEAD_DIM] = 1.0
    tok = lambda w: pl.BlockSpec((1, TM_OUT, w), lambda i, j: (i, j, 0))
    full = lambda shp: pl.BlockSpec(shp, lambda i, j: (0,) * len(shp))
    return pl.pallas_call(
        _merge_kernel,
        grid=(b, t // TM_OUT),
        in_specs=[tok(D_MODEL), tok(GROUP_W), tok(GROUP_W), tok(GROUP_W), tok(LANES), tok(LANES), tok(LANES),
                  tok(WIDTH_A), tok(WIDTH_B), tok(WIDTH_B), tok(GATE_W),
                  full((LANES, GROUP_W)), full((WIDTH_A, D_MODEL)), full((WIDTH_B, D_MODEL)),
                  full((D_MODEL, D_MODEL)), full((2, D_MODEL)), full((1, D_MODEL))],
        out_specs=tok(D_MODEL),
        out_shape=jax.ShapeDtypeStruct((b, t, D_MODEL), x.dtype),
        compiler_params=pltpu.CompilerParams(
            dimension_semantics=("arbitrary", "arbitrary"), vmem_limit_bytes=VMEM_LIMIT),
        name="merge",
    )(x, *oas, *lses, za, ob, zb, gates, jnp.asarray(expand, jnp.bfloat16),
      w_proj_a.astype(jnp.bfloat16), w_proj_b.astype(jnp.bfloat16), w_out.astype(jnp.bfloat16),
      b_gate, norm_post.reshape(1, D_MODEL))


def _encoder_layer(x, norm_pre, w_perm, b_gate, nbr_bias, w_proj_a, w_proj_b, w_out, norm_post):
    qkv_a0, qkv_a1, qkv_a2, za, qkv_b, zb, gates = _inproj_call(x, norm_pre, w_perm)
    oas, lses = zip(*(_dilated_call(q, g) for g, q in enumerate((qkv_a0, qkv_a1, qkv_a2))))
    ob = _nbr_call(qkv_b, nbr_bias)
    return _merge_call(x, oas, lses, za, ob, zb, gates, b_gate, w_proj_a, w_proj_b, w_out, norm_post)


def kernel(x_prompt, x_sample, norm_pre, w_in, b_gate, rpb, w_proj_a, w_proj_b, w_out, norm_post):
    y_prompt, y_sample = x_prompt, x_sample
    for l in range(norm_pre.shape[0]):
        w_perm = _permute_w_in(w_in[l])
        nbr_bias = _nbr_bias_table(rpb[l])
        args = (norm_pre[l], w_perm, b_gate[l], nbr_bias, w_proj_a[l], w_proj_b[l], w_out[l], norm_post[l])
        y_prompt = _encoder_layer(y_prompt, *args)
        y_sample = _encoder_layer(y_sample, *args)
    return (y_prompt, y_sample)
```

```python
import functools
import math

import numpy as np
import jax
import jax.numpy as jnp
from jax import lax
from jax.experimental import pallas as pl
from jax.experimental.pallas import tpu as pltpu

D_MODEL = 1024
SEQ = 2048
HEAD_DIM = 64
DIL_PATTERNS = ((128, 1), (512, 4), (2048, 16))
N_GROUPS_A = 3
HEADS_PER_GROUP_A = 6
N_HEADS_A = N_GROUPS_A * HEADS_PER_GROUP_A
WIDTH_A = N_HEADS_A * HEAD_DIM
GROUP_W = HEADS_PER_GROUP_A * HEAD_DIM
N_HEADS_B = 14
WIDTH_B = N_HEADS_B * HEAD_DIM
GRID_W = 64
GRID_ROWS = SEQ // GRID_W
NA_KH = 8
NA_KW = 16
RMS_EPS = 1e-6
NEG_INF = -1e30

LANES = 128
PAIR_W = 2 * HEAD_DIM
PAIRS_A = HEADS_PER_GROUP_A // 2
PAIRS_B = N_HEADS_B // 2
HALF_WIN = 64
QBLK_A = 128
KWIN_A = QBLK_A + 2 * HALF_WIN
NBR_KEYS = NA_KH * GRID_W
VMEM_LIMIT = 56 * 1024 * 1024

QKV_A_W = 3 * GROUP_W
QKV_B_W = 3 * WIDTH_B
GATE_W = 2 * D_MODEL
OUT_WIDTHS = (QKV_A_W, QKV_A_W, QKV_A_W, WIDTH_A, QKV_B_W, WIDTH_B, GATE_W)
IN_COLS = sum(OUT_WIDTHS)
PROJ_CHUNK = 512
TM_IN = 512
TM_OUT = 512


def _permute_w_in(w_in):
    scale = HEAD_DIM ** -0.5
    qa, ka, va, za = (w_in[:, i * WIDTH_A:(i + 1) * WIDTH_A] for i in range(4))
    ob = 4 * WIDTH_A
    qb, kb, vb, zb = (w_in[:, ob + i * WIDTH_B: ob + (i + 1) * WIDTH_B] for i in range(4))
    gates = w_in[:, ob + 4 * WIDTH_B:]
    cols = []
    for g in range(N_GROUPS_A):
        sl = slice(g * GROUP_W, (g + 1) * GROUP_W)
        cols += [qa[:, sl] * scale, ka[:, sl], va[:, sl]]
    cols += [za, qb * scale, kb, vb, zb, gates]
    return jnp.concatenate(cols, axis=1).astype(jnp.bfloat16)


def _inproj_kernel(x_ref, g_ref, w_ref, *out_refs):
    x = x_ref[0]
    ms = jnp.mean(x * x, axis=-1, keepdims=True)
    hn = (x * lax.rsqrt(ms + RMS_EPS) * g_ref[...]).astype(jnp.bfloat16)
    bounds = np.cumsum((0,) + OUT_WIDTHS)
    for c0 in range(0, IN_COLS, PROJ_CHUNK):
        c1 = c0 + PROJ_CHUNK
        res = jnp.dot(hn, w_ref[:, c0:c1], preferred_element_type=jnp.float32).astype(jnp.bfloat16)
        for oi, o_ref in enumerate(out_refs):
            lo, hi = max(c0, int(bounds[oi])), min(c1, int(bounds[oi + 1]))
            if lo < hi:
                o_ref[0, :, lo - int(bounds[oi]):hi - int(bounds[oi])] = res[:, lo - c0:hi - c0]


def _inproj_call(x, norm_pre, w_perm):
    b, t, _ = x.shape
    out_shape = [jax.ShapeDtypeStruct((b, t, w), jnp.bfloat16) for w in OUT_WIDTHS]
    out_specs = [pl.BlockSpec((1, TM_IN, w), lambda i, j: (i, j, 0)) for w in OUT_WIDTHS]
    return pl.pallas_call(
        _inproj_kernel,
        grid=(b, t // TM_IN),
        in_specs=[
            pl.BlockSpec((1, TM_IN, D_MODEL), lambda i, j: (i, j, 0)),
            pl.BlockSpec((1, D_MODEL), lambda i, j: (0, 0)),
            pl.BlockSpec((D_MODEL, IN_COLS), lambda i, j: (0, 0), pipeline_mode=pl.Buffered(1)),
        ],
        out_specs=out_specs,
        out_shape=out_shape,
        compiler_params=pltpu.CompilerParams(
            dimension_semantics=("arbitrary", "arbitrary"), vmem_limit_bytes=VMEM_LIMIT),
        name="inproj",
    )(x, norm_pre.reshape(1, D_MODEL), w_perm)


def _alibi_slopes():
    return (2.0 ** (-8.0 * np.arange(1, N_HEADS_A + 1) / N_HEADS_A)).astype(np.float32)


def _dilated_bias_table(g, seq_len):
    _, d = DIL_PATTERNS[g]
    sub_len = seq_len // d
    kwin = min(KWIN_A, sub_len)
    deltas = (-HALF_WIN, 0, sub_len - kwin - (sub_len - QBLK_A)) if sub_len > QBLK_A else (0,)
    slopes = _alibi_slopes()[g::N_GROUPS_A]
    i = np.arange(QBLK_A)[:, None]
    j = np.arange(kwin)[None, :]
    tab = np.empty((PAIRS_A, len(deltas), 2 * QBLK_A, kwin), np.float32)
    for vi, delta in enumerate(deltas):
        rel = np.abs(j - i + delta)
        for p in range(PAIRS_A):
            for hh in range(2):
                bias = np.where(rel <= HALF_WIN, -slopes[2 * p + hh] * (rel * d).astype(np.float32), NEG_INF)
                tab[p, vi, hh * QBLK_A:(hh + 1) * QBLK_A] = bias
    return jnp.asarray(tab)


def _stack_heads(q2):
    lane = lax.broadcasted_iota(jnp.int32, q2.shape, 1)
    zero = jnp.zeros_like(q2)
    return jnp.concatenate([jnp.where(lane < HEAD_DIM, q2, zero), jnp.where(lane < HEAD_DIM, zero, q2)], axis=0)


def _score_dot(qs, k2):
    return lax.dot_general(qs, k2, (((1,), (1,)), ((), ())), preferred_element_type=jnp.float32)


def _dilated_kernel(qkv_ref, bias_ref, o_ref, lse_ref, s_buf, p_buf, r_buf, lse_buf, *, sub_len, n_res):
    kwin = min(KWIN_A, sub_len)
    n_blk = sub_len // QBLK_A
    assert n_blk == 1 or n_res == 1
    n_units = n_blk * n_res
    lane = lax.broadcasted_iota(jnp.int32, (QBLK_A, LANES), 1)

    def unit(j):
        if n_blk == 1:
            return j, 0, 0, 0
        if isinstance(j, int):
            k0 = min(max(j * QBLK_A - HALF_WIN, 0), sub_len - kwin)
            return 0, (1 if j == 0 else 2 if j == n_blk - 1 else 0), j * QBLK_A, k0
        k0 = pl.multiple_of(jnp.clip(j * QBLK_A - HALF_WIN, 0, sub_len - kwin), HALF_WIN)
        return 0, jnp.where(j == 0, 1, jnp.where(j == n_blk - 1, 2, 0)), pl.multiple_of(j * QBLK_A, QBLK_A), k0

    def scores(j, p, slot):
        res, _, q0, k0 = unit(j)
        c = res * QKV_A_W + p * PAIR_W
        qs = _stack_heads(qkv_ref[0, pl.ds(q0, QBLK_A), c:c + PAIR_W])
        s_buf[slot] = _score_dot(qs, qkv_ref[0, pl.ds(k0, kwin), c + GROUP_W:c + GROUP_W + PAIR_W])

    def softmax(j, p, slot, half):
        res, var, q0, _ = unit(j)
        s = s_buf[slot] + bias_ref[p, var]
        m = jnp.max(s, axis=-1, keepdims=True)
        e = jnp.exp(s - m)
        l = jnp.sum(e, axis=-1, keepdims=True)
        p_buf[slot] = e.astype(p_buf.dtype)
        r_buf[slot] = jnp.broadcast_to(1.0 / l, (2 * QBLK_A, LANES))
        lse = m + jnp.log(l)
        tile = jnp.zeros((QBLK_A, LANES), jnp.float32) if p == 0 else lse_buf[half]
        tile = jnp.where(lane == 2 * p, lse[:QBLK_A], tile)
        tile = jnp.where(lane == 2 * p + 1, lse[QBLK_A:], tile)
        if p == PAIRS_A - 1:
            lse_ref[0, pl.ds(q0, QBLK_A), res * LANES:(res + 1) * LANES] = tile
        else:
            lse_buf[half] = tile

    def weighted_values(j, p, slot):
        res, _, q0, k0 = unit(j)
        c = res * QKV_A_W + p * PAIR_W + 2 * GROUP_W
        pv = jnp.dot(p_buf[slot], qkv_ref[0, pl.ds(k0, kwin), c:c + PAIR_W], preferred_element_type=jnp.float32)
        pvn = pv * r_buf[slot]
        oc = res * GROUP_W + p * PAIR_W
        o_ref[0, pl.ds(q0, QBLK_A), oc:oc + PAIR_W] = (
            jnp.where(lane < HEAD_DIM, pvn[:QBLK_A], pvn[QBLK_A:]).astype(o_ref.dtype))

    def two_units(t, first=False, last=False):
        items = [(2 * t + u, p) for u in range(2) for p in range(PAIRS_A)]
        prev_item, next_item = (2 * t - 1, PAIRS_A - 1), (2 * t + 2, 0)
        for i, (j, p) in enumerate(items):
            slot = i % 2
            if i + 1 < len(items):
                scores(*items[i + 1], 1 - slot)
            elif not last:
                scores(*next_item, 1 - slot)
            if i > 0:
                weighted_values(*items[i - 1], 1 - slot)
            elif not first:
                weighted_values(*prev_item, 1 - slot)
            softmax(j, p, slot, half=i // PAIRS_A)

    n_iter = n_units // 2
    scores(0, 0, 0)
    two_units(0, first=True, last=n_iter == 1)
    if n_iter > 2:
        def body(t, carry):
            two_units(t)
            return carry
        lax.fori_loop(1, n_iter - 1, body, 0)
    if n_iter > 1:
        two_units(n_iter - 1, last=True)
    weighted_values(n_units - 1, PAIRS_A - 1, 1)


def _dilated_call(qkv, g):
    b, t, _ = qkv.shape
    _, d = DIL_PATTERNS[g]
    sub_len = t // d
    n_res = min(d, 4) if sub_len == QBLK_A else 1
    kwin = min(KWIN_A, sub_len)
    bias = _dilated_bias_table(g, t)
    qkv_v = qkv.reshape(b, sub_len, d * QKV_A_W)
    o, lse = pl.pallas_call(
        functools.partial(_dilated_kernel, sub_len=sub_len, n_res=n_res),
        grid=(b, d // n_res),
        in_specs=[
            pl.BlockSpec((1, sub_len, n_res * QKV_A_W), lambda i, j: (i, 0, j)),
            pl.BlockSpec(bias.shape, lambda i, j: (0, 0, 0, 0)),
        ],
        out_specs=[
            pl.BlockSpec((1, sub_len, n_res * GROUP_W), lambda i, j: (i, 0, j)),
            pl.BlockSpec((1, sub_len, n_res * LANES), lambda i, j: (i, 0, j)),
        ],
        out_shape=[
            jax.ShapeDtypeStruct((b, sub_len, d * GROUP_W), jnp.bfloat16),
            jax.ShapeDtypeStruct((b, sub_len, d * LANES), jnp.float32),
        ],
        scratch_shapes=[pltpu.VMEM((2, 2 * QBLK_A, kwin), jnp.float32),
                        pltpu.VMEM((2, 2 * QBLK_A, kwin), jnp.bfloat16),
                        pltpu.VMEM((2, 2 * QBLK_A, LANES), jnp.float32),
                        pltpu.VMEM((2, QBLK_A, LANES), jnp.float32)],
        compiler_params=pltpu.CompilerParams(
            dimension_semantics=("arbitrary", "arbitrary"), vmem_limit_bytes=VMEM_LIMIT),
        name=f"dilated_g{g}",
    )(qkv_v, bias)
    return o.reshape(b, t, GROUP_W), lse.reshape(b, t, LANES)


def _nbr_bias_table(rpb):
    rpb = rpb.astype(jnp.float32)
    n_dc = 2 * NA_KW - 1
    pad = GRID_W - NA_KW
    ext = jnp.concatenate([jnp.repeat(rpb[..., :1], pad, axis=-1), rpb,
                           jnp.repeat(rpb[..., n_dc - 1:], pad, axis=-1)], axis=-1)
    toep = jnp.stack([ext[..., GRID_W - 1 - q: 2 * GRID_W - 1 - q] for q in range(GRID_W)], axis=-2)
    qc = np.arange(GRID_W)[:, None]
    kc = np.arange(GRID_W)[None, :]
    cs = np.clip(qc - NA_KW // 2, 0, GRID_W - NA_KW)
    valid = (kc >= cs) & (kc < cs + NA_KW)
    toep = jnp.where(valid, toep, NEG_INF)
    tab = jnp.stack([toep[:, v:v + NA_KH] for v in range(NA_KH)], axis=1)
    tab = tab.reshape(PAIRS_B, 2, NA_KH, NA_KH, GRID_W, GRID_W).transpose(0, 2, 1, 4, 3, 5)
    return tab.reshape(PAIRS_B, NA_KH, 2 * GRID_W, NBR_KEYS)


def _nbr_kernel(q_ref, k_ref, v_ref, bias_ref, o_ref, s_buf, p_buf, r_buf):
    lane = lax.broadcasted_iota(jnp.int32, (GRID_W, LANES), 1)

    def rows(r):
        if isinstance(r, int):
            rs = min(max(r - NA_KH // 2, 0), GRID_ROWS - NA_KH)
            return rs - r + NA_KH - 1, r * GRID_W, rs * GRID_W
        rs = jnp.clip(r - NA_KH // 2, 0, GRID_ROWS - NA_KH)
        return (rs - r + NA_KH - 1, pl.multiple_of(r * GRID_W, GRID_W), pl.multiple_of(rs * GRID_W, GRID_W))

    def scores(r, slot):
        _, q0, k0 = rows(r)
        qs = _stack_heads(q_ref[0, pl.ds(q0, GRID_W), :])
        s_buf[slot] = lax.dot_general(qs, k_ref[0, pl.ds(k0, NBR_KEYS), :], (((1,), (1,)), ((), ())),
                                      preferred_element_type=jnp.float32)

    def softmax(r, slot):
        var, _, _ = rows(r)
        s = s_buf[slot] + bias_ref[0, var]
        e = jnp.exp(s - jnp.max(s, axis=-1, keepdims=True))
        p_buf[slot] = e.astype(p_buf.dtype)
        r_buf[slot] = jnp.broadcast_to(1.0 / jnp.sum(e, axis=-1, keepdims=True), (2 * GRID_W, LANES))

    def weighted_values(r, slot):
        _, q0, k0 = rows(r)
        pv = jnp.dot(p_buf[slot], v_ref[0, pl.ds(k0, NBR_KEYS), :], preferred_element_type=jnp.float32)
        pvn = pv * r_buf[slot]
        o_ref[0, pl.ds(q0, GRID_W), :] = jnp.where(lane < HEAD_DIM, pvn[:GRID_W], pvn[GRID_W:]).astype(o_ref.dtype)

    def two_rows(j, first=False, last=False):
        a, b = 2 * j, 2 * j + 1
        scores(b, 1)
        if not first:
            weighted_values(a - 1, 1)
        softmax(a, 0)
        if not last:
            scores(b + 1, 0)
        weighted_values(a, 0)
        softmax(b, 1)

    n_iter = GRID_ROWS // 2
    scores(0, 0)
    two_rows(0, first=True)

    def body(j, carry):
        two_rows(j)
        return carry

    lax.fori_loop(1, n_iter - 1, body, 0)
    two_rows(n_iter - 1, last=True)
    weighted_values(GRID_ROWS - 1, 1)


def _nbr_call(qkv, bias):
    b, t, _ = qkv.shape
    return pl.pallas_call(
        _nbr_kernel,
        grid=(PAIRS_B, b),
        in_specs=[
            pl.BlockSpec((1, t, PAIR_W), lambda p, i: (i, 0, p)),
            pl.BlockSpec((1, t, PAIR_W), lambda p, i: (i, 0, PAIRS_B + p)),
            pl.BlockSpec((1, t, PAIR_W), lambda p, i: (i, 0, 2 * PAIRS_B + p)),
            pl.BlockSpec((1, NA_KH, 2 * GRID_W, NBR_KEYS), lambda p, i: (p, 0, 0, 0)),
        ],
        out_specs=pl.BlockSpec((1, t, PAIR_W), lambda p, i: (i, 0, p)),
        out_shape=jax.ShapeDtypeStruct((b, t, WIDTH_B), jnp.bfloat16),
        scratch_shapes=[pltpu.VMEM((2, 2 * GRID_W, NBR_KEYS), jnp.float32),
                        pltpu.VMEM((2, 2 * GRID_W, NBR_KEYS), jnp.bfloat16),
                        pltpu.VMEM((2, 2 * GRID_W, LANES), jnp.float32)],
        compiler_params=pltpu.CompilerParams(
            dimension_semantics=("arbitrary", "arbitrary"), vmem_limit_bytes=VMEM_LIMIT),
        name="nbr",
    )(qkv, qkv, qkv, bias)


def _silu(z):
    return z * jax.nn.sigmoid(z)


def _merge_kernel(x_ref, oa0_ref, oa1_ref, oa2_ref, l0_ref, l1_ref, l2_ref, za_ref, ob_ref, zb_ref, g_ref,
                  expand_ref, wpa_ref, wpb_ref, wout_ref, bg_ref, npost_ref, y_ref):
    f32, bf16 = jnp.float32, jnp.bfloat16
    lses = [r[0] for r in (l0_ref, l1_ref, l2_ref)]
    mx = jnp.maximum(jnp.maximum(lses[0], lses[1]), lses[2])
    es = [jnp.exp(l - mx) for l in lses]
    inv = 1.0 / (es[0] + es[1] + es[2])
    ua = []
    for g, (e, oa_ref) in enumerate(zip(es, (oa0_ref, oa1_ref, oa2_ref))):
        alpha = e * inv
        hi = alpha.astype(bf16)
        lo = (alpha - hi.astype(f32)).astype(bf16)
        aexp = (jnp.dot(hi, expand_ref[...], preferred_element_type=f32)
                + jnp.dot(lo, expand_ref[...], preferred_element_type=f32))
        za = za_ref[0, :, g * GROUP_W:(g + 1) * GROUP_W].astype(f32)
        ua.append((oa_ref[0].astype(f32) * aexp * _silu(za)).astype(bf16))
    ya = jnp.dot(jnp.concatenate(ua, axis=1), wpa_ref[...], preferred_element_type=f32)
    ub = (ob_ref[0].astype(f32) * _silu(zb_ref[0].astype(f32))).astype(bf16)
    yb = jnp.dot(ub, wpb_ref[...], preferred_element_type=f32)
    ga = g_ref[0, :, :D_MODEL].astype(f32) + bg_ref[0:1, :]
    gb = g_ref[0, :, D_MODEL:].astype(f32) + bg_ref[1:2, :]
    merged = jax.nn.sigmoid(ga) * ya + jax.nn.sigmoid(gb) * yb
    out = jnp.dot(merged.astype(bf16), wout_ref[...], preferred_element_type=f32)
    ms = jnp.mean(out * out, axis=-1, keepdims=True)
    y_ref[0] = x_ref[0] + out * lax.rsqrt(ms + RMS_EPS) * npost_ref[...]


def _merge_call(x, oas, lses, za, ob, zb, gates, b_gate, w_proj_a, w_proj_b, w_out, norm_post):
    b, t, _ = x.shape
    expand = np.zeros((LANES, GROUP_W), np.float32)
    for i in range(HEADS_PER_GROUP_A):
        expand[i, i * HEAD_DIM:(i + 1) * HEAD_DIM] = 1.0
    tok = lambda w: pl.BlockSpec((1, TM_OUT, w), lambda i, j: (i, j, 0))
    full = lambda shp: pl.BlockSpec(shp, lambda i, j: (0,) * len(shp))
    return pl.pallas_call(
        _merge_kernel,
        grid=(b, t // TM_OUT),
        in_specs=[tok(D_MODEL), tok(GROUP_W), tok(GROUP_W), tok(GROUP_W), tok(LANES), tok(LANES), tok(LANES),
                  tok(WIDTH_A), tok(WIDTH_B), tok(WIDTH_B), tok(GATE_W),
                  full((LANES, GROUP_W)), full((WIDTH_A, D_MODEL)), full((WIDTH_B, D_MODEL)),
                  full((D_MODEL, D_MODEL)), full((2, D_MODEL)), full((1, D_MODEL))],
        out_specs=tok(D_MODEL),
        out_shape=jax.ShapeDtypeStruct((b, t, D_MODEL), x.dtype),
        compiler_params=pltpu.CompilerParams(
            dimension_semantics=("arbitrary", "arbitrary"), vmem_limit_bytes=VMEM_LIMIT),
        name="merge",
    )(x, *oas, *lses, za, ob, zb, gates, jnp.asarray(expand, jnp.bfloat16),
      w_proj_a.astype(jnp.bfloat16), w_proj_b.astype(jnp.bfloat16), w_out.astype(jnp.bfloat16),
      b_gate, norm_post.reshape(1, D_MODEL))


def _encoder_layer(x, norm_pre, w_perm, b_gate, nbr_bias, w_proj_a, w_proj_b, w_out, norm_post):
    qkv_a0, qkv_a1, qkv_a2, za, qkv_b, zb, gates = _inproj_call(x, norm_pre, w_perm)
    oas, lses = zip(*(_dilated_call(q, g) for g, q in enumerate((qkv_a0, qkv_a1, qkv_a2))))
    ob = _nbr_call(qkv_b, nbr_bias)
    return _merge_call(x, oas, lses, za, ob, zb, gates, b_gate, w_proj_a, w_proj_b, w_out, norm_post)


def kernel(x_prompt, x_sample, norm_pre, w_in, b_gate, rpb, w_proj_a, w_proj_b, w_out, norm_post):
    y_prompt, y_sample = x_prompt, x_sample
    for l in range(norm_pre.shape[0]):
        w_perm = _permute_w_in(w_in[l])
        nbr_bias = _nbr_bias_table(rpb[l])
        args = (norm_pre[l], w_perm, b_gate[l], nbr_bias, w_proj_a[l], w_proj_b[l], w_out[l], norm_post[l])
        y_prompt = _encoder_layer(y_prompt, *args)
        y_sample = _encoder_layer(y_sample, *args)
    return (y_prompt, y_sample)
```

```python
import functools
import math

import numpy as np
import jax
import jax.numpy as jnp
from jax import lax
from jax.experimental import pallas as pl
from jax.experimental.pallas import tpu as pltpu

D_MODEL = 1024
SEQ = 2048
HEAD_DIM = 64
DIL_PATTERNS = ((128, 1), (512, 4), (2048, 16))
DILATIONS = tuple(d for _, d in DIL_PATTERNS)
N_GROUPS_A = 3
HEADS_PER_GROUP_A = 6
N_HEADS_A = N_GROUPS_A * HEADS_PER_GROUP_A
WIDTH_A = N_HEADS_A * HEAD_DIM
GROUP_W = HEADS_PER_GROUP_A * HEAD_DIM
N_HEADS_B = 14
WIDTH_B = N_HEADS_B * HEAD_DIM
GRID_W = 64
GRID_ROWS = SEQ // GRID_W
NA_KH = 8
NA_KW = 16
RMS_EPS = 1e-6
NEG_INF = -1e30
LOG2E = math.log2(math.e)

LANES = 128
N_SLABS = D_MODEL // LANES
PAIR_W = 2 * HEAD_DIM
PAIRS_A = HEADS_PER_GROUP_A // 2
PAIRS_B = N_HEADS_B // 2
HALF_WIN = 64
QBLK_A = 128
KWIN_A = QBLK_A + 2 * HALF_WIN
NBR_KEYS = NA_KH * GRID_W
SOFTMAX_CHUNK = 32 * 1024
NBR_ROWS_PER_ITER = 16
DIL_UNITS_PER_STEP = 16
VMEM_LIMIT = 56 * 1024 * 1024

QKV_A_W = 3 * GROUP_W
SEG_A_W = 4 * GROUP_W
SEG_B_W = 4 * WIDTH_B
GATE_W = 2 * D_MODEL
IN_COLS = N_GROUPS_A * SEG_A_W + SEG_B_W + GATE_W
PROJ_CHUNK = 512
TM_IN = 512
TM_OUT = 512


def _permute_w_in(w_in):
    scale = LOG2E * HEAD_DIM ** -0.5
    qa, ka, va, za = (w_in[:, i * WIDTH_A:(i + 1) * WIDTH_A] for i in range(4))
    ob = 4 * WIDTH_A
    qb, kb, vb, zb = (w_in[:, ob + i * WIDTH_B: ob + (i + 1) * WIDTH_B] for i in range(4))
    gates = w_in[:, ob + 4 * WIDTH_B:]
    cols = []
    for g in range(N_GROUPS_A):
        sl = slice(g * GROUP_W, (g + 1) * GROUP_W)
        cols += [qa[:, sl] * scale, ka[:, sl], va[:, sl], za[:, sl]]
    cols += [qb * scale, kb, vb, zb, gates]
    return jnp.concatenate(cols, axis=1).astype(jnp.bfloat16)


def _inproj_kernel(*refs):
    x_refs = refs[:N_SLABS]
    g_ref, w_ref = refs[N_SLABS:N_SLABS + 2]
    out_refs = refs[N_SLABS + 2:N_SLABS + 7]
    xperm_ref = refs[N_SLABS + 7]
    hn_refs = refs[N_SLABS + 8:N_SLABS + 11]

    def normalise(slabs, hn_ref):
        ssq = sum(jnp.sum(s * s, axis=-1, keepdims=True) for s in slabs)
        rstd = lax.rsqrt(ssq * (1.0 / D_MODEL) + RMS_EPS)
        for c, s in enumerate(slabs):
            cols = slice(c * LANES, (c + 1) * LANES)
            hn_ref[:, cols] = (s * rstd * g_ref[:, cols]).astype(hn_ref.dtype)

    def permuted(d, hn_ref):
        n = TM_IN // d
        for c in range(N_SLABS):
            for r in range(d):
                xperm_ref[c, r * n:(r + 1) * n, :] = x_refs[c][0, pl.ds(r, n, stride=d), :]
        normalise([xperm_ref[c] for c in range(N_SLABS)], hn_ref)

    starts = np.cumsum([0] + [SEG_A_W] * N_GROUPS_A + [SEG_B_W])
    segments = [(out_refs[3], 1, SEG_B_W, starts[3]), (out_refs[4], 1, GATE_W, starts[4]),
                (out_refs[0], 1, SEG_A_W, starts[0])]
    segments += [(out_refs[g], DILATIONS[g], SEG_A_W, starts[g]) for g in range(1, N_GROUPS_A)]
    normalise([r[0] for r in x_refs], hn_refs[0])
    for o_ref, d, width, col in segments:
        n = TM_IN // d
        hn_ref = hn_refs[DILATIONS.index(d)]
        if d > 1:
            permuted(d, hn_ref)
        for a in range(0, width, PROJ_CHUNK):
            c0 = int(col) + a
            res = jnp.dot(hn_ref[...], w_ref[:, c0:c0 + PROJ_CHUNK],
                          preferred_element_type=jnp.float32).astype(jnp.bfloat16)
            if d == 1:
                o_ref[0, :, a:a + PROJ_CHUNK] = res
            else:
                for r in range(d):
                    o_ref[0, r, :, a:a + PROJ_CHUNK] = res[r * n:(r + 1) * n]


def _inproj_call(x, norm_pre, w_perm):
    b, t, _ = x.shape
    bf16 = jnp.bfloat16
    out_shape, out_specs = [jax.ShapeDtypeStruct((b, t, SEG_A_W), bf16)], [
        pl.BlockSpec((1, TM_IN, SEG_A_W), lambda i, j: (i, j, 0))]
    for d in DILATIONS[1:]:
        out_shape.append(jax.ShapeDtypeStruct((b, d, t // d, SEG_A_W), bf16))
        out_specs.append(pl.BlockSpec((1, d, TM_IN // d, SEG_A_W), lambda i, j: (i, 0, j, 0)))
    for w in (SEG_B_W, GATE_W):
        out_shape.append(jax.ShapeDtypeStruct((b, t, w), bf16))
        out_specs.append(pl.BlockSpec((1, TM_IN, w), lambda i, j: (i, j, 0)))
    x_specs = [pl.BlockSpec((1, TM_IN, LANES), functools.partial(lambda i, j, c: (i, j, c), c=c))
               for c in range(N_SLABS)]
    return pl.pallas_call(
        _inproj_kernel,
        grid=(b, t // TM_IN),
        in_specs=x_specs + [
            pl.BlockSpec((1, D_MODEL), lambda i, j: (0, 0)),
            pl.BlockSpec((D_MODEL, IN_COLS), lambda i, j: (0, 0), pipeline_mode=pl.Buffered(1)),
        ],
        out_specs=out_specs,
        out_shape=out_shape,
        scratch_shapes=[pltpu.VMEM((N_SLABS, TM_IN, LANES), jnp.float32)]
        + [pltpu.VMEM((TM_IN, D_MODEL), bf16)] * N_GROUPS_A,
        compiler_params=pltpu.CompilerParams(
            dimension_semantics=("arbitrary", "arbitrary"), vmem_limit_bytes=VMEM_LIMIT),
        name="inproj",
    )(*([x] * N_SLABS), norm_pre.reshape(1, D_MODEL), w_perm)


def _stack_heads(q2):
    lane = lax.broadcasted_iota(jnp.int32, q2.shape, 1)
    zero = jnp.zeros_like(q2)
    return jnp.concatenate([jnp.where(lane < HEAD_DIM, q2, zero), jnp.where(lane < HEAD_DIM, zero, q2)], axis=0)


def _score_dot(qs, k2):
    return lax.dot_general(qs, k2, (((1,), (1,)), ((), ())), preferred_element_type=jnp.float32)


def _softmax_rows(s_ref, bias_ref, p_ref, r_ref, n_rows):
    chunk = SOFTMAX_CHUNK // s_ref.shape[-1]
    for r0 in range(0, n_rows, chunk):
        rows = slice(r0, r0 + chunk)
        s = s_ref[rows, :] + bias_ref[rows, :]
        m = jnp.max(s, axis=-1, keepdims=True)
        e = jnp.exp2(s - m)
        l = jnp.sum(e, axis=-1, keepdims=True)
        p_ref[rows, :] = e.astype(p_ref.dtype)
        r_ref[rows, :] = jnp.broadcast_to(1.0 / l, (chunk, LANES))
        yield rows, m, l


def _alibi_slopes():
    return (2.0 ** (-8.0 * np.arange(1, N_HEADS_A + 1) / N_HEADS_A)).astype(np.float32)


def _dilated_bias_table(g, seq_len):
    d = DILATIONS[g]
    sub_len = seq_len // d
    kwin = min(KWIN_A, sub_len)
    deltas = (-HALF_WIN, 0, QBLK_A - kwin) if sub_len > QBLK_A else (0,)
    slopes = _alibi_slopes()[g::N_GROUPS_A]
    i = np.arange(QBLK_A)[:, None]
    j = np.arange(kwin)[None, :]
    tab = np.empty((PAIRS_A, len(deltas), 2 * QBLK_A, kwin), np.float32)
    for vi, delta in enumerate(deltas):
        rel = np.abs(j - i + delta)
        for p in range(PAIRS_A):
            for hh in range(2):
                bias = np.where(rel <= HALF_WIN,
                                np.float32(LOG2E) * (-slopes[2 * p + hh] * (rel * d).astype(np.float32)), NEG_INF)
                tab[p, vi, hh * QBLK_A:(hh + 1) * QBLK_A] = bias
    return jnp.asarray(tab)


def _dilated_kernel(qkv_ref, bias_ref, o_ref, lse_ref, s_buf, p_buf, r_buf, lse_buf, *, sub_len, n_res):
    kwin = min(KWIN_A, sub_len)
    n_blk = sub_len // QBLK_A
    n_units = n_blk * n_res
    lane = lax.broadcasted_iota(jnp.int32, (QBLK_A, LANES), 1)

    def unit(j):
        res, n = divmod(j, n_blk)
        if n_blk == 1:
            return res, 0, 0, 0
        k0 = min(max(n * QBLK_A - HALF_WIN, 0), sub_len - kwin)
        return res, (1 if n == 0 else 2 if n == n_blk - 1 else 0), n * QBLK_A, k0

    def scores(j, p, slot):
        res, _, q0, k0 = unit(j)
        c = p * PAIR_W
        qs = _stack_heads(qkv_ref[0, res, pl.ds(q0, QBLK_A), c:c + PAIR_W])
        s_buf[slot] = _score_dot(qs, qkv_ref[0, res, pl.ds(k0, kwin), c + GROUP_W:c + GROUP_W + PAIR_W])

    def softmax(j, p, slot):
        _, var, _, _ = unit(j)
        for rows, m, l in _softmax_rows(s_buf.at[slot], bias_ref.at[p, var], p_buf.at[slot], r_buf.at[slot],
                                        2 * QBLK_A):
            lse_buf[slot, rows, :] = jnp.broadcast_to(m + jnp.log2(l), (m.shape[0], LANES))

    def weighted_values(j, p, slot):
        res, _, q0, k0 = unit(j)
        c = p * PAIR_W + 2 * GROUP_W
        pv = jnp.dot(p_buf[slot], qkv_ref[0, res, pl.ds(k0, kwin), c:c + PAIR_W],
                     preferred_element_type=jnp.float32)
        pvn = pv * r_buf[slot]
        o_ref[0, res, pl.ds(q0, QBLK_A), p * PAIR_W:(p + 1) * PAIR_W] = (
            jnp.where(lane < HEAD_DIM, pvn[:QBLK_A], pvn[QBLK_A:]).astype(o_ref.dtype))
        lse = lse_buf[slot]
        tile = jnp.zeros((QBLK_A, LANES), jnp.float32) if p == 0 else lse_ref[0, res, pl.ds(q0, QBLK_A), :]
        tile = jnp.where(lane == 2 * p, lse[:QBLK_A], tile)
        lse_ref[0, res, pl.ds(q0, QBLK_A), :] = jnp.where(lane == 2 * p + 1, lse[QBLK_A:], tile)

    items = [(j, p) for j in range(n_units) for p in range(PAIRS_A)]
    scores(*items[0], 0)
    for i, item in enumerate(items):
        slot = i % 2
        if i + 1 < len(items):
            scores(*items[i + 1], 1 - slot)
        if i > 0:
            weighted_values(*items[i - 1], 1 - slot)
        softmax(*item, slot)
    weighted_values(*items[-1], (len(items) - 1) % 2)


def _dilated_call(seg, g):
    b, d, sub_len, _ = seg.shape
    n_res = min(d, DIL_UNITS_PER_STEP * QBLK_A // sub_len)
    kwin = min(KWIN_A, sub_len)
    bias = _dilated_bias_table(g, d * sub_len)
    return pl.pallas_call(
        functools.partial(_dilated_kernel, sub_len=sub_len, n_res=n_res),
        grid=(b, d // n_res),
        in_specs=[
            pl.BlockSpec((1, n_res, sub_len, QKV_A_W), lambda i, j: (i, j, 0, 0)),
            pl.BlockSpec(bias.shape, lambda i, j: (0, 0, 0, 0)),
        ],
        out_specs=[
            pl.BlockSpec((1, n_res, sub_len, GROUP_W), lambda i, j: (i, j, 0, 0)),
            pl.BlockSpec((1, n_res, sub_len, LANES), lambda i, j: (i, j, 0, 0)),
        ],
        out_shape=[
            jax.ShapeDtypeStruct((b, d, sub_len, GROUP_W), jnp.bfloat16),
            jax.ShapeDtypeStruct((b, d, sub_len, LANES), jnp.float32),
        ],
        scratch_shapes=[pltpu.VMEM((2, 2 * QBLK_A, kwin), jnp.float32),
                        pltpu.VMEM((2, 2 * QBLK_A, kwin), jnp.bfloat16),
                        pltpu.VMEM((2, 2 * QBLK_A, LANES), jnp.float32),
                        pltpu.VMEM((2, 2 * QBLK_A, LANES), jnp.float32)],
        compiler_params=pltpu.CompilerParams(
            dimension_semantics=("arbitrary", "arbitrary"), vmem_limit_bytes=VMEM_LIMIT),
        name=f"dilated_g{g}",
    )(seg, bias)


def _nbr_bias_table(rpb):
    rpb = rpb.astype(jnp.float32) * LOG2E
    n_dc = 2 * NA_KW - 1
    pad = GRID_W - NA_KW
    ext = jnp.concatenate([jnp.repeat(rpb[..., :1], pad, axis=-1), rpb,
                           jnp.repeat(rpb[..., n_dc - 1:], pad, axis=-1)], axis=-1)
    toep = jnp.stack([ext[..., GRID_W - 1 - q: 2 * GRID_W - 1 - q] for q in range(GRID_W)], axis=-2)
    qc = np.arange(GRID_W)[:, None]
    kc = np.arange(GRID_W)[None, :]
    cs = np.clip(qc - NA_KW // 2, 0, GRID_W - NA_KW)
    valid = (kc >= cs) & (kc < cs + NA_KW)
    toep = jnp.where(valid, toep, NEG_INF)
    tab = jnp.stack([toep[:, v:v + NA_KH] for v in range(NA_KH)], axis=1)
    tab = tab.reshape(PAIRS_B, 2, NA_KH, NA_KH, GRID_W, GRID_W).transpose(0, 2, 1, 4, 3, 5)
    return tab.reshape(PAIRS_B, NA_KH, 2 * GRID_W, NBR_KEYS)


def _nbr_kernel(q_ref, k_ref, v_ref, bias_ref, o_ref, s_buf, p_buf, r_buf):
    lane = lax.broadcasted_iota(jnp.int32, (GRID_W, LANES), 1)

    def rows(r):
        if isinstance(r, int):
            rs = min(max(r - NA_KH // 2, 0), GRID_ROWS - NA_KH)
            return rs - r + NA_KH - 1, r * GRID_W, rs * GRID_W
        rs = jnp.clip(r - NA_KH // 2, 0, GRID_ROWS - NA_KH)
        return (rs - r + NA_KH - 1, pl.multiple_of(r * GRID_W, GRID_W), pl.multiple_of(rs * GRID_W, GRID_W))

    def scores(r, slot):
        _, q0, k0 = rows(r)
        s_buf[slot] = _score_dot(_stack_heads(q_ref[0, pl.ds(q0, GRID_W), :]), k_ref[0, pl.ds(k0, NBR_KEYS), :])

    def softmax(r, slot):
        var, _, _ = rows(r)
        for _ in _softmax_rows(s_buf.at[slot], bias_ref.at[0, var], p_buf.at[slot], r_buf.at[slot], 2 * GRID_W):
            pass

    def weighted_values(r, slot):
        _, q0, k0 = rows(r)
        pv = jnp.dot(p_buf[slot], v_ref[0, pl.ds(k0, NBR_KEYS), :], preferred_element_type=jnp.float32)
        pvn = pv * r_buf[slot]
        o_ref[0, pl.ds(q0, GRID_W), :] = jnp.where(lane < HEAD_DIM, pvn[:GRID_W], pvn[GRID_W:]).astype(o_ref.dtype)

    def row_group(j, first=False, last=False):
        for u in range(NBR_ROWS_PER_ITER):
            r, slot = j * NBR_ROWS_PER_ITER + u, u % 2
            if not (last and u == NBR_ROWS_PER_ITER - 1):
                scores(r + 1, 1 - slot)
            if not (first and u == 0):
                weighted_values(r - 1, 1 - slot)
            softmax(r, slot)

    n_iter = GRID_ROWS // NBR_ROWS_PER_ITER
    scores(0, 0)
    row_group(0, first=True)

    def body(j, carry):
        row_group(j)
        return carry

    if n_iter > 2:
        lax.fori_loop(1, n_iter - 1, body, 0)
    row_group(n_iter - 1, last=True)
    weighted_values(GRID_ROWS - 1, 1)


def _nbr_call(seg, bias):
    b, t, _ = seg.shape
    return pl.pallas_call(
        _nbr_kernel,
        grid=(PAIRS_B, b),
        in_specs=[
            pl.BlockSpec((1, t, PAIR_W), lambda p, i: (i, 0, p)),
            pl.BlockSpec((1, t, PAIR_W), lambda p, i: (i, 0, PAIRS_B + p)),
            pl.BlockSpec((1, t, PAIR_W), lambda p, i: (i, 0, 2 * PAIRS_B + p)),
            pl.BlockSpec((1, NA_KH, 2 * GRID_W, NBR_KEYS), lambda p, i: (p, 0, 0, 0)),
        ],
        out_specs=pl.BlockSpec((1, t, PAIR_W), lambda p, i: (i, 0, p)),
        out_shape=jax.ShapeDtypeStruct((b, t, WIDTH_B), jnp.bfloat16),
        scratch_shapes=[pltpu.VMEM((2, 2 * GRID_W, NBR_KEYS), jnp.float32),
                        pltpu.VMEM((2, 2 * GRID_W, NBR_KEYS), jnp.bfloat16),
                        pltpu.VMEM((2, 2 * GRID_W, LANES), jnp.float32)],
        compiler_params=pltpu.CompilerParams(
            dimension_semantics=("arbitrary", "arbitrary"), vmem_limit_bytes=VMEM_LIMIT),
        name="nbr",
    )(seg, seg, seg, bias)


def _silu(z):
    return z * jax.nn.sigmoid(z)


def _merge_kernel(x_ref, o0_ref, o1_ref, o2_ref, l0_ref, l1_ref, l2_ref, z0_ref, z1_ref, z2_ref, ob_ref, zb_ref,
                  g_ref, expand_ref, wpa_ref, wpb_ref, wout_ref, bg_ref, npost_ref, y_ref, lse_nat, gated_nat):
    f32, bf16 = jnp.float32, jnp.bfloat16
    tm = x_ref.shape[1]
    lses = [l0_ref[0]]
    gated = [o0_ref[0].astype(f32) * _silu(z0_ref[0].astype(f32))]
    for gi, (d, o_ref, l_ref, z_ref) in enumerate(zip(DILATIONS[1:], (o1_ref, o2_ref), (l1_ref, l2_ref),
                                                        (z1_ref, z2_ref))):
        n = tm // d
        for r in range(d):
            nat_rows = pl.ds(r, n, stride=d)
            lse_nat[gi, nat_rows, :] = l_ref[0, r]
            w = o_ref[0, r].astype(f32) * _silu(z_ref[0, r].astype(f32))
            for c in range(GROUP_W // LANES):
                gated_nat[gi, c, nat_rows, :] = w[:, c * LANES:(c + 1) * LANES]
        lses.append(lse_nat[gi])
        gated.append(jnp.concatenate([gated_nat[gi, c] for c in range(GROUP_W // LANES)], axis=1))
    mx = jnp.maximum(jnp.maximum(lses[0], lses[1]), lses[2])
    es = [jnp.exp2(l - mx) for l in lses]
    inv = 1.0 / (es[0] + es[1] + es[2])
    lane = lax.broadcasted_iota(jnp.int32, (tm, LANES), 1)
    alphas = [jnp.where(lane < HEADS_PER_GROUP_A, e * inv, 0.0) for e in es]
    comb = alphas[0] + pltpu.roll(alphas[1], HEADS_PER_GROUP_A, 1) + pltpu.roll(alphas[2], 2 * HEADS_PER_GROUP_A, 1)
    hi = comb.astype(bf16)
    lo = (comb - hi.astype(f32)).astype(bf16)
    aexp = jnp.dot(jnp.concatenate([hi, lo], axis=1), expand_ref[...], preferred_element_type=f32)
    ua = (jnp.concatenate(gated, axis=1) * aexp).astype(bf16)
    ya = jnp.dot(ua, wpa_ref[...], preferred_element_type=f32)
    ub = (ob_ref[0].astype(f32) * _silu(zb_ref[0].astype(f32))).astype(bf16)
    yb = jnp.dot(ub, wpb_ref[...], preferred_element_type=f32)
    ga = g_ref[0, :, :D_MODEL].astype(f32) + bg_ref[0:1, :]
    gb = g_ref[0, :, D_MODEL:].astype(f32) + bg_ref[1:2, :]
    merged = jax.nn.sigmoid(ga) * ya + jax.nn.sigmoid(gb) * yb
    out = jnp.dot(merged.astype(bf16), wout_ref[...], preferred_element_type=f32)
    ms = jnp.mean(out * out, axis=-1, keepdims=True)
    y_ref[0] = x_ref[0] + out * lax.rsqrt(ms + RMS_EPS) * npost_ref[...]


def _merge_call(x, segs_a, oas, lses, ob, seg_b, gates, b_gate, w_proj_a, w_proj_b, w_out, norm_post):
    b, t, _ = x.shape
    expand = np.zeros((2 * LANES, WIDTH_A), np.float32)
    for h in range(N_HEADS_A):
        expand[h, h * HEAD_DIM:(h + 1) * HEAD_DIM] = 1.0
        expand[LANES + h, h * HEAD_DIM:(h + 1) * HEAD_DIM] = 1.0
    z_col = QKV_A_W // GROUP_W

    def tok(w, col=0):
        return pl.BlockSpec((1, TM_OUT, w), lambda i, j: (i, j, col))

    def res_major(d, w, col=0):
        return pl.BlockSpec((1, d, TM_OUT // d, w), lambda i, j: (i, 0, j, col))

    full = lambda shp: pl.BlockSpec(shp, lambda i, j: (0,) * len(shp))
    d1, d2 = DILATIONS[1:]
    return pl.pallas_call(
        _merge_kernel,
        grid=(b, t // TM_OUT),
        in_specs=[tok(D_MODEL),
                  tok(GROUP_W), res_major(d1, GROUP_W), res_major(d2, GROUP_W),
                  tok(LANES), res_major(d1, LANES), res_major(d2, LANES),
                  tok(GROUP_W, z_col), res_major(d1, GROUP_W, z_col), res_major(d2, GROUP_W, z_col),
                  tok(WIDTH_B), tok(WIDTH_B, 3), tok(GATE_W),
                  full((2 * LANES, WIDTH_A)), full((WIDTH_A, D_MODEL)), full((WIDTH_B, D_MODEL)),
                  full((D_MODEL, D_MODEL)), full((2, D_MODEL)), full((1, D_MODEL))],
        out_specs=tok(D_MODEL),
        out_shape=jax.ShapeDtypeStruct((b, t, D_MODEL), x.dtype),
        scratch_shapes=[pltpu.VMEM((2, TM_OUT, LANES), jnp.float32),
                        pltpu.VMEM((2, GROUP_W // LANES, TM_OUT, LANES), jnp.float32)],
        compiler_params=pltpu.CompilerParams(
            dimension_semantics=("arbitrary", "arbitrary"), vmem_limit_bytes=VMEM_LIMIT),
        name="merge",
    )(x, *oas, *lses, *segs_a, ob, seg_b, gates, jnp.asarray(expand, jnp.bfloat16),
      w_proj_a.astype(jnp.bfloat16), w_proj_b.astype(jnp.bfloat16), w_out.astype(jnp.bfloat16),
      b_gate, norm_post.reshape(1, D_MODEL))


def _encoder_layer(x, norm_pre, w_perm, b_gate, nbr_bias, w_proj_a, w_proj_b, w_out, norm_post):
    b, t, _ = x.shape
    seg_a0, seg_a1, seg_a2, seg_b, gates = _inproj_call(x, norm_pre, w_perm)
    segs_a = (seg_a0, seg_a1, seg_a2)
    att = [_dilated_call(seg.reshape(b, DILATIONS[g], t // DILATIONS[g], SEG_A_W), g)
           for g, seg in enumerate(segs_a)]
    oas = [att[0][0].reshape(b, t, GROUP_W), att[1][0], att[2][0]]
    lses = [att[0][1].reshape(b, t, LANES), att[1][1], att[2][1]]
    ob = _nbr_call(seg_b, nbr_bias)
    return _merge_call(x, segs_a, oas, lses, ob, seg_b, gates, b_gate, w_proj_a, w_proj_b, w_out, norm_post)


def kernel(x_prompt, x_sample, norm_pre, w_in, b_gate, rpb, w_proj_a, w_proj_b, w_out, norm_post):
    y_prompt, y_sample = x_prompt, x_sample
    for l in range(norm_pre.shape[0]):
        w_perm = _permute_w_in(w_in[l])
        nbr_bias = _nbr_bias_table(rpb[l])
        args = (norm_pre[l], w_perm, b_gate[l], nbr_bias, w_proj_a[l], w_proj_b[l], w_out[l], norm_post[l])
        y_prompt = _encoder_layer(y_prompt, *args)
        y_sample = _encoder_layer(y_sample, *args)
    return (y_prompt, y_sample)
```

```python
import functools
import math

import numpy as np
import jax
import jax.numpy as jnp
from jax import lax
from jax.experimental import pallas as pl
from jax.experimental.pallas import tpu as pltpu

D_MODEL = 1024
SEQ = 2048
HEAD_DIM = 64
DIL_PATTERNS = ((128, 1), (512, 4), (2048, 16))
DILATIONS = tuple(d for _, d in DIL_PATTERNS)
N_GROUPS_A = 3
HEADS_PER_GROUP_A = 6
N_HEADS_A = N_GROUPS_A * HEADS_PER_GROUP_A
WIDTH_A = N_HEADS_A * HEAD_DIM
GROUP_W = HEADS_PER_GROUP_A * HEAD_DIM
N_HEADS_B = 14
WIDTH_B = N_HEADS_B * HEAD_DIM
GRID_W = 64
GRID_ROWS = SEQ // GRID_W
NA_KH = 8
NA_KW = 16
RMS_EPS = 1e-6
NEG_INF = -1e30
LOG2E = math.log2(math.e)

LANES = 128
N_SLABS = D_MODEL // LANES
PAIR_W = 2 * HEAD_DIM
PAIRS_A = HEADS_PER_GROUP_A // 2
PAIRS_B = N_HEADS_B // 2
HALF_WIN = 64
QBLK_A = 128
KWIN_A = QBLK_A + 2 * HALF_WIN
NBR_KEYS = NA_KH * GRID_W
SOFTMAX_CHUNK = 32 * 1024
NBR_ROWS_PER_ITER = 16
DIL_UNITS_PER_STEP = 16
VMEM_LIMIT = 56 * 1024 * 1024

QKV_A_W = 3 * GROUP_W
SEG_A_W = 4 * GROUP_W
SEG_B_W = 4 * WIDTH_B
GATE_W = 2 * D_MODEL
IN_COLS = N_GROUPS_A * SEG_A_W + SEG_B_W + GATE_W
PROJ_CHUNK = 512
TM_IN = 512
TM_OUT = 512
MERGE_SPLIT = 1


def _permute_w_in(w_in):
    scale = LOG2E * HEAD_DIM ** -0.5
    qa, ka, va, za = (w_in[:, i * WIDTH_A:(i + 1) * WIDTH_A] for i in range(4))
    ob = 4 * WIDTH_A
    qb, kb, vb, zb = (w_in[:, ob + i * WIDTH_B: ob + (i + 1) * WIDTH_B] for i in range(4))
    gates = w_in[:, ob + 4 * WIDTH_B:]
    cols = []
    for g in range(N_GROUPS_A):
        sl = slice(g * GROUP_W, (g + 1) * GROUP_W)
        cols += [qa[:, sl] * scale, ka[:, sl], va[:, sl], za[:, sl]]
    cols += [qb * scale, kb, vb, zb, gates]
    return jnp.concatenate(cols, axis=1).astype(jnp.bfloat16)


def _inproj_kernel(*refs):
    x_refs = refs[:N_SLABS]
    g_ref, w_ref = refs[N_SLABS:N_SLABS + 2]
    out_refs = refs[N_SLABS + 2:N_SLABS + 7]
    xperm_ref = refs[N_SLABS + 7]
    hn_refs = refs[N_SLABS + 8:N_SLABS + 11]

    def normalise(slabs, hn_ref):
        ssq = sum(jnp.sum(s * s, axis=-1, keepdims=True) for s in slabs)
        rstd = lax.rsqrt(ssq * (1.0 / D_MODEL) + RMS_EPS)
        for c, s in enumerate(slabs):
            cols = slice(c * LANES, (c + 1) * LANES)
            hn_ref[:, cols] = (s * rstd * g_ref[:, cols]).astype(hn_ref.dtype)

    def permuted(d, hn_ref):
        n = TM_IN // d
        for c in range(N_SLABS):
            for r in range(d):
                xperm_ref[c, r * n:(r + 1) * n, :] = x_refs[c][0, pl.ds(r, n, stride=d), :]
        normalise([xperm_ref[c] for c in range(N_SLABS)], hn_ref)

    starts = np.cumsum([0] + [SEG_A_W] * N_GROUPS_A + [SEG_B_W])
    segments = [(out_refs[3], 1, SEG_B_W, starts[3]), (out_refs[4], 1, GATE_W, starts[4]),
                (out_refs[0], 1, SEG_A_W, starts[0])]
    segments += [(out_refs[g], DILATIONS[g], SEG_A_W, starts[g]) for g in range(1, N_GROUPS_A)]
    normalise([r[0] for r in x_refs], hn_refs[0])
    for o_ref, d, width, col in segments:
        n = TM_IN // d
        hn_ref = hn_refs[DILATIONS.index(d)]
        if d > 1:
            permuted(d, hn_ref)
        for a in range(0, width, PROJ_CHUNK):
            c0 = int(col) + a
            res = jnp.dot(hn_ref[...], w_ref[:, c0:c0 + PROJ_CHUNK],
                          preferred_element_type=jnp.float32).astype(jnp.bfloat16)
            if d == 1:
                o_ref[0, :, a:a + PROJ_CHUNK] = res
            else:
                for r in range(d):
                    o_ref[0, r, :, a:a + PROJ_CHUNK] = res[r * n:(r + 1) * n]


def _inproj_call(x, norm_pre, w_perm):
    b, t, _ = x.shape
    bf16 = jnp.bfloat16
    out_shape, out_specs = [jax.ShapeDtypeStruct((b, t, SEG_A_W), bf16)], [
        pl.BlockSpec((1, TM_IN, SEG_A_W), lambda i, j: (i, j, 0))]
    for d in DILATIONS[1:]:
        out_shape.append(jax.ShapeDtypeStruct((b, d, t // d, SEG_A_W), bf16))
        out_specs.append(pl.BlockSpec((1, d, TM_IN // d, SEG_A_W), lambda i, j: (i, 0, j, 0)))
    for w in (SEG_B_W, GATE_W):
        out_shape.append(jax.ShapeDtypeStruct((b, t, w), bf16))
        out_specs.append(pl.BlockSpec((1, TM_IN, w), lambda i, j: (i, j, 0)))
    x_specs = [pl.BlockSpec((1, TM_IN, LANES), functools.partial(lambda i, j, c: (i, j, c), c=c))
               for c in range(N_SLABS)]
    return pl.pallas_call(
        _inproj_kernel,
        grid=(b, t // TM_IN),
        in_specs=x_specs + [
            pl.BlockSpec((1, D_MODEL), lambda i, j: (0, 0)),
            pl.BlockSpec((D_MODEL, IN_COLS), lambda i, j: (0, 0), pipeline_mode=pl.Buffered(1)),
        ],
        out_specs=out_specs,
        out_shape=out_shape,
        scratch_shapes=[pltpu.VMEM((N_SLABS, TM_IN, LANES), jnp.float32)]
        + [pltpu.VMEM((TM_IN, D_MODEL), bf16)] * N_GROUPS_A,
        compiler_params=pltpu.CompilerParams(
            dimension_semantics=("arbitrary", "arbitrary"), vmem_limit_bytes=VMEM_LIMIT),
        name="inproj",
    )(*([x] * N_SLABS), norm_pre.reshape(1, D_MODEL), w_perm)


def _stack_heads(q2):
    lane = lax.broadcasted_iota(jnp.int32, q2.shape, 1)
    zero = jnp.zeros_like(q2)
    return jnp.concatenate([jnp.where(lane < HEAD_DIM, q2, zero), jnp.where(lane < HEAD_DIM, zero, q2)], axis=0)


def _score_dot(qs, k2):
    return lax.dot_general(qs, k2, (((1,), (1,)), ((), ())), preferred_element_type=jnp.float32)


def _softmax_rows(s_ref, bias_ref, p_ref, n_rows):
    chunk = SOFTMAX_CHUNK // s_ref.shape[-1]
    maxima = []
    for r0 in range(0, n_rows, chunk):
        rows = slice(r0, r0 + chunk)
        s = s_ref[rows, :] + bias_ref[rows, :]
        m = jnp.max(s, axis=-1, keepdims=True)
        p_ref[rows, :] = jnp.exp2(s - m).astype(p_ref.dtype)
        maxima.append(jnp.broadcast_to(m, (chunk, LANES)))
    return jnp.concatenate(maxima, axis=0)


def _values_with_ones(p, v2):
    vx = jnp.concatenate([v2, jnp.ones_like(v2)], axis=1)
    pvx = jnp.dot(p, vx, preferred_element_type=jnp.float32)
    return pvx[:, :LANES], pvx[:, LANES:]


def _alibi_slopes():
    return (2.0 ** (-8.0 * np.arange(1, N_HEADS_A + 1) / N_HEADS_A)).astype(np.float32)


def _dilated_bias_table(g, seq_len):
    d = DILATIONS[g]
    sub_len = seq_len // d
    kwin = min(KWIN_A, sub_len)
    deltas = (-HALF_WIN, 0, QBLK_A - kwin) if sub_len > QBLK_A else (0,)
    slopes = _alibi_slopes()[g::N_GROUPS_A]
    i = np.arange(QBLK_A)[:, None]
    j = np.arange(kwin)[None, :]
    tab = np.empty((PAIRS_A, len(deltas), 2 * QBLK_A, kwin), np.float32)
    for vi, delta in enumerate(deltas):
        rel = np.abs(j - i + delta)
        for p in range(PAIRS_A):
            for hh in range(2):
                bias = np.where(rel <= HALF_WIN,
                                np.float32(LOG2E) * (-slopes[2 * p + hh] * (rel * d).astype(np.float32)), NEG_INF)
                tab[p, vi, hh * QBLK_A:(hh + 1) * QBLK_A] = bias
    return jnp.asarray(tab)


def _dilated_kernel(qkv_ref, bias_ref, o_ref, st_ref, s_buf, p_buf, m_scr, l_scr, *, sub_len, n_res):
    kwin = min(KWIN_A, sub_len)
    n_blk = sub_len // QBLK_A
    n_units = n_blk * n_res
    lane = lax.broadcasted_iota(jnp.int32, (QBLK_A, LANES), 1)

    def unit(j):
        res, n = divmod(j, n_blk)
        if n_blk == 1:
            return res, 0, 0, 0
        k0 = min(max(n * QBLK_A - HALF_WIN, 0), sub_len - kwin)
        return res, (1 if n == 0 else 2 if n == n_blk - 1 else 0), n * QBLK_A, k0

    def scores(j, p, slot):
        res, _, q0, k0 = unit(j)
        c = p * PAIR_W
        qs = _stack_heads(qkv_ref[0, res, pl.ds(q0, QBLK_A), c:c + PAIR_W])
        s_buf[slot] = _score_dot(qs, qkv_ref[0, res, pl.ds(k0, kwin), c + GROUP_W:c + GROUP_W + PAIR_W])

    def head_lanes(tile_ref, j, p, first_lane, col):
        tile = jnp.zeros((QBLK_A, LANES), jnp.float32) if p == 0 else tile_ref[j % 2]
        tile = jnp.where(lane == first_lane + 2 * p, col[:QBLK_A], tile)
        return jnp.where(lane == first_lane + 2 * p + 1, col[QBLK_A:], tile)

    def softmax(j, p, slot):
        _, var, _, _ = unit(j)
        m = _softmax_rows(s_buf.at[slot], bias_ref.at[p, var], p_buf.at[slot], 2 * QBLK_A)
        m_scr[j % 2] = head_lanes(m_scr, j, p, 0, m)

    def weighted_values(j, p, slot):
        res, _, q0, k0 = unit(j)
        c = p * PAIR_W + 2 * GROUP_W
        pv, l = _values_with_ones(p_buf[slot], qkv_ref[0, res, pl.ds(k0, kwin), c:c + PAIR_W])
        o_ref[0, res, pl.ds(q0, QBLK_A), p * PAIR_W:(p + 1) * PAIR_W] = (
            jnp.where(lane < HEAD_DIM, pv[:QBLK_A], pv[QBLK_A:]).astype(o_ref.dtype))
        sums = head_lanes(l_scr, j, p, HEADS_PER_GROUP_A, l)
        if p == PAIRS_A - 1:
            st_ref[0, res, pl.ds(q0, QBLK_A), :] = jnp.where(lane < HEADS_PER_GROUP_A, m_scr[j % 2], sums)
        else:
            l_scr[j % 2] = sums

    items = [(j, p) for j in range(n_units) for p in range(PAIRS_A)]
    scores(*items[0], 0)
    for i, item in enumerate(items):
        slot = i % 2
        if i + 1 < len(items):
            scores(*items[i + 1], 1 - slot)
        if i > 0:
            weighted_values(*items[i - 1], 1 - slot)
        softmax(*item, slot)
    weighted_values(*items[-1], (len(items) - 1) % 2)


def _dilated_call(seg, g):
    b, d, sub_len, _ = seg.shape
    n_res = min(d, DIL_UNITS_PER_STEP * QBLK_A // sub_len)
    kwin = min(KWIN_A, sub_len)
    bias = _dilated_bias_table(g, d * sub_len)
    return pl.pallas_call(
        functools.partial(_dilated_kernel, sub_len=sub_len, n_res=n_res),
        grid=(b, d // n_res),
        in_specs=[
            pl.BlockSpec((1, n_res, sub_len, QKV_A_W), lambda i, j: (i, j, 0, 0)),
            pl.BlockSpec(bias.shape, lambda i, j: (0, 0, 0, 0)),
        ],
        out_specs=[
            pl.BlockSpec((1, n_res, sub_len, GROUP_W), lambda i, j: (i, j, 0, 0)),
            pl.BlockSpec((1, n_res, sub_len, LANES), lambda i, j: (i, j, 0, 0)),
        ],
        out_shape=[
            jax.ShapeDtypeStruct((b, d, sub_len, GROUP_W), jnp.bfloat16),
            jax.ShapeDtypeStruct((b, d, sub_len, LANES), jnp.float32),
        ],
        scratch_shapes=[pltpu.VMEM((2, 2 * QBLK_A, kwin), jnp.float32),
                        pltpu.VMEM((2, 2 * QBLK_A, kwin), jnp.bfloat16),
                        pltpu.VMEM((2, QBLK_A, LANES), jnp.float32),
                        pltpu.VMEM((2, QBLK_A, LANES), jnp.float32)],
        compiler_params=pltpu.CompilerParams(
            dimension_semantics=("arbitrary", "arbitrary"), vmem_limit_bytes=VMEM_LIMIT),
        name=f"dilated_g{g}",
    )(seg, bias)


def _nbr_bias_table(rpb):
    rpb = rpb.astype(jnp.float32) * LOG2E
    n_dc = 2 * NA_KW - 1
    pad = GRID_W - NA_KW
    ext = jnp.concatenate([jnp.repeat(rpb[..., :1], pad, axis=-1), rpb,
                           jnp.repeat(rpb[..., n_dc - 1:], pad, axis=-1)], axis=-1)
    toep = jnp.stack([ext[..., GRID_W - 1 - q: 2 * GRID_W - 1 - q] for q in range(GRID_W)], axis=-2)
    qc = np.arange(GRID_W)[:, None]
    kc = np.arange(GRID_W)[None, :]
    cs = np.clip(qc - NA_KW // 2, 0, GRID_W - NA_KW)
    valid = (kc >= cs) & (kc < cs + NA_KW)
    toep = jnp.where(valid, toep, NEG_INF)
    tab = jnp.stack([toep[:, v:v + NA_KH] for v in range(NA_KH)], axis=1)
    tab = tab.reshape(PAIRS_B, 2, NA_KH, NA_KH, GRID_W, GRID_W).transpose(0, 2, 1, 4, 3, 5)
    return tab.reshape(PAIRS_B, NA_KH, 2 * GRID_W, NBR_KEYS)


def _nbr_kernel(q_ref, k_ref, v_ref, bias_ref, o_ref, s_buf, p_buf):
    lane = lax.broadcasted_iota(jnp.int32, (GRID_W, LANES), 1)

    def rows(r):
        if isinstance(r, int):
            rs = min(max(r - NA_KH // 2, 0), GRID_ROWS - NA_KH)
            return rs - r + NA_KH - 1, r * GRID_W, rs * GRID_W
        rs = jnp.clip(r - NA_KH // 2, 0, GRID_ROWS - NA_KH)
        return (rs - r + NA_KH - 1, pl.multiple_of(r * GRID_W, GRID_W), pl.multiple_of(rs * GRID_W, GRID_W))

    def scores(r, slot):
        _, q0, k0 = rows(r)
        s_buf[slot] = _score_dot(_stack_heads(q_ref[0, pl.ds(q0, GRID_W), :]), k_ref[0, pl.ds(k0, NBR_KEYS), :])

    def softmax(r, slot):
        var, _, _ = rows(r)
        _softmax_rows(s_buf.at[slot], bias_ref.at[0, var], p_buf.at[slot], 2 * GRID_W)

    def weighted_values(r, slot):
        _, q0, k0 = rows(r)
        pv, l = _values_with_ones(p_buf[slot], v_ref[0, pl.ds(k0, NBR_KEYS), :])
        pvn = pv * (1.0 / l)
        o_ref[0, pl.ds(q0, GRID_W), :] = jnp.where(lane < HEAD_DIM, pvn[:GRID_W], pvn[GRID_W:]).astype(o_ref.dtype)

    def row_group(j, first=False, last=False):
        for u in range(NBR_ROWS_PER_ITER):
            r, slot = j * NBR_ROWS_PER_ITER + u, u % 2
            if not (last and u == NBR_ROWS_PER_ITER - 1):
                scores(r + 1, 1 - slot)
            if not (first and u == 0):
                weighted_values(r - 1, 1 - slot)
            softmax(r, slot)

    n_iter = GRID_ROWS // NBR_ROWS_PER_ITER
    scores(0, 0)
    row_group(0, first=True)

    def body(j, carry):
        row_group(j)
        return carry

    if n_iter > 2:
        lax.fori_loop(1, n_iter - 1, body, 0)
    row_group(n_iter - 1, last=True)
    weighted_values(GRID_ROWS - 1, 1)


def _nbr_call(seg, bias):
    b, t, _ = seg.shape
    return pl.pallas_call(
        _nbr_kernel,
        grid=(PAIRS_B, b),
        in_specs=[
            pl.BlockSpec((1, t, PAIR_W), lambda p, i: (i, 0, p)),
            pl.BlockSpec((1, t, PAIR_W), lambda p, i: (i, 0, PAIRS_B + p)),
            pl.BlockSpec((1, t, PAIR_W), lambda p, i: (i, 0, 2 * PAIRS_B + p)),
            pl.BlockSpec((1, NA_KH, 2 * GRID_W, NBR_KEYS), lambda p, i: (p, 0, 0, 0)),
        ],
        out_specs=pl.BlockSpec((1, t, PAIR_W), lambda p, i: (i, 0, p)),
        out_shape=jax.ShapeDtypeStruct((b, t, WIDTH_B), jnp.bfloat16),
        scratch_shapes=[pltpu.VMEM((2, 2 * GRID_W, NBR_KEYS), jnp.float32),
                        pltpu.VMEM((2, 2 * GRID_W, NBR_KEYS), jnp.bfloat16)],
        compiler_params=pltpu.CompilerParams(
            dimension_semantics=("arbitrary", "arbitrary"), vmem_limit_bytes=VMEM_LIMIT),
        name="nbr",
    )(seg, seg, seg, bias)


def _silu(z):
    return z * jax.nn.sigmoid(z)


def _merge_kernel(x_ref, o0_ref, o1_ref, o2_ref, st0_ref, st1_ref, st2_ref, z0_ref, z1_ref, z2_ref, ob_ref, zb_ref,
                  g_ref, expand_ref, wpa_ref, wpb_ref, wout_ref, bg_ref, npost_ref, y_ref, stats_nat, gated_nat):
    f32, bf16 = jnp.float32, jnp.bfloat16
    hm = x_ref.shape[1] // MERGE_SPLIT
    lane = lax.broadcasted_iota(jnp.int32, (hm, LANES), 1)
    n_slabs = GROUP_W // LANES

    for h in range(MERGE_SPLIT):
        rows = slice(h * hm, (h + 1) * hm)
        ub = (ob_ref[0, rows, :].astype(f32) * _silu(zb_ref[0, rows, :].astype(f32))).astype(bf16)
        yb = jnp.dot(ub, wpb_ref[...], preferred_element_type=f32)

        stats = [st0_ref[0, rows, :]]
        gated = [o0_ref[0, rows, :].astype(f32) * _silu(z0_ref[0, rows, :].astype(f32))]
        for gi, (d, o_ref, st_ref, z_ref) in enumerate(zip(DILATIONS[1:], (o1_ref, o2_ref), (st1_ref, st2_ref),
                                                             (z1_ref, z2_ref))):
            n = hm // d
            sub = slice(h * n, (h + 1) * n)
            for r in range(d):
                nat_rows = pl.ds(r, n, stride=d)
                stats_nat[h, gi, nat_rows, :] = st_ref[0, r, sub, :]
                w = o_ref[0, r, sub, :].astype(f32) * _silu(z_ref[0, r, sub, :].astype(f32))
                for c in range(n_slabs):
                    gated_nat[h, gi, c, nat_rows, :] = w[:, c * LANES:(c + 1) * LANES]
            stats.append(stats_nat[h, gi])
            gated.append(jnp.concatenate([gated_nat[h, gi, c] for c in range(n_slabs)], axis=1))
        sums = [pltpu.roll(st, LANES - HEADS_PER_GROUP_A, 1) for st in stats]
        mx = jnp.maximum(jnp.maximum(stats[0], stats[1]), stats[2])
        es = [jnp.exp2(st - mx) for st in stats]
        inv = 1.0 / (sums[0] * es[0] + sums[1] * es[1] + sums[2] * es[2])
        weights = [jnp.where(lane < HEADS_PER_GROUP_A, e * inv, 0.0) for e in es]
        comb = (weights[0] + pltpu.roll(weights[1], HEADS_PER_GROUP_A, 1)
                + pltpu.roll(weights[2], 2 * HEADS_PER_GROUP_A, 1))
        hi = comb.astype(bf16)
        lo = (comb - hi.astype(f32)).astype(bf16)
        wexp = jnp.dot(jnp.concatenate([hi, lo], axis=1), expand_ref[...], preferred_element_type=f32)
        ua = (jnp.concatenate(gated, axis=1) * wexp).astype(bf16)
        ya = jnp.dot(ua, wpa_ref[...], preferred_element_type=f32)
        ga = g_ref[0, rows, :D_MODEL].astype(f32) + bg_ref[0:1, :]
        gb = g_ref[0, rows, D_MODEL:].astype(f32) + bg_ref[1:2, :]
        merged = jax.nn.sigmoid(ga) * ya + jax.nn.sigmoid(gb) * yb
        out = jnp.dot(merged.astype(bf16), wout_ref[...], preferred_element_type=f32)
        ms = jnp.mean(out * out, axis=-1, keepdims=True)
        y_ref[0, rows, :] = x_ref[0, rows, :] + out * lax.rsqrt(ms + RMS_EPS) * npost_ref[...]


def _merge_call(x, segs_a, oas, stats, ob, seg_b, gates, b_gate, w_proj_a, w_proj_b, w_out, norm_post):
    b, t, _ = x.shape
    expand = np.zeros((2 * LANES, WIDTH_A), np.float32)
    for h in range(N_HEADS_A):
        expand[h, h * HEAD_DIM:(h + 1) * HEAD_DIM] = 1.0
        expand[LANES + h, h * HEAD_DIM:(h + 1) * HEAD_DIM] = 1.0
    z_col = QKV_A_W // GROUP_W

    def tok(w, col=0):
        return pl.BlockSpec((1, TM_OUT, w), lambda i, j: (i, j, col))

    def res_major(d, w, col=0):
        return pl.BlockSpec((1, d, TM_OUT // d, w), lambda i, j: (i, 0, j, col))

    full = lambda shp: pl.BlockSpec(shp, lambda i, j: (0,) * len(shp))
    d1, d2 = DILATIONS[1:]
    return pl.pallas_call(
        _merge_kernel,
        grid=(b, t // TM_OUT),
        in_specs=[tok(D_MODEL),
                  tok(GROUP_W), res_major(d1, GROUP_W), res_major(d2, GROUP_W),
                  tok(LANES), res_major(d1, LANES), res_major(d2, LANES),
                  tok(GROUP_W, z_col), res_major(d1, GROUP_W, z_col), res_major(d2, GROUP_W, z_col),
                  tok(WIDTH_B), tok(WIDTH_B, 3), tok(GATE_W),
                  full((2 * LANES, WIDTH_A)), full((WIDTH_A, D_MODEL)), full((WIDTH_B, D_MODEL)),
                  full((D_MODEL, D_MODEL)), full((2, D_MODEL)), full((1, D_MODEL))],
        out_specs=tok(D_MODEL),
        out_shape=jax.ShapeDtypeStruct((b, t, D_MODEL), x.dtype),
        scratch_shapes=[pltpu.VMEM((MERGE_SPLIT, 2, TM_OUT // MERGE_SPLIT, LANES), jnp.float32),
                        pltpu.VMEM((MERGE_SPLIT, 2, GROUP_W // LANES, TM_OUT // MERGE_SPLIT, LANES), jnp.float32)],
        compiler_params=pltpu.CompilerParams(
            dimension_semantics=("arbitrary", "arbitrary"), vmem_limit_bytes=VMEM_LIMIT),
        name="merge",
    )(x, *oas, *stats, *segs_a, ob, seg_b, gates, jnp.asarray(expand, jnp.bfloat16),
      w_proj_a.astype(jnp.bfloat16), w_proj_b.astype(jnp.bfloat16), w_out.astype(jnp.bfloat16),
      b_gate, norm_post.reshape(1, D_MODEL))


def _encoder_layer(x, norm_pre, w_perm, b_gate, nbr_bias, w_proj_a, w_proj_b, w_out, norm_post):
    b, t, _ = x.shape
    seg_a0, seg_a1, seg_a2, seg_b, gates = _inproj_call(x, norm_pre, w_perm)
    segs_a = (seg_a0, seg_a1, seg_a2)
    att = [_dilated_call(seg.reshape(b, DILATIONS[g], t // DILATIONS[g], SEG_A_W), g)
           for g, seg in enumerate(segs_a)]
    oas = [att[0][0].reshape(b, t, GROUP_W), att[1][0], att[2][0]]
    stats = [att[0][1].reshape(b, t, LANES), att[1][1], att[2][1]]
    ob = _nbr_call(seg_b, nbr_bias)
    return _merge_call(x, segs_a, oas, stats, ob, seg_b, gates, b_gate, w_proj_a, w_proj_b, w_out, norm_post)


def kernel(x_prompt, x_sample, norm_pre, w_in, b_gate, rpb, w_proj_a, w_proj_b, w_out, norm_post):
    y_prompt, y_sample = x_prompt, x_sample
    for l in range(norm_pre.shape[0]):
        w_perm = _permute_w_in(w_in[l])
        nbr_bias = _nbr_bias_table(rpb[l])
        args = (norm_pre[l], w_perm, b_gate[l], nbr_bias, w_proj_a[l], w_proj_b[l], w_out[l], norm_post[l])
        y_prompt = _encoder_layer(y_prompt, *args)
        y_sample = _encoder_layer(y_sample, *args)
    return (y_prompt, y_sample)
```

```python
import functools
import math

import numpy as np
import jax
import jax.numpy as jnp
from jax import lax
from jax.experimental import pallas as pl
from jax.experimental.pallas import tpu as pltpu

D_MODEL = 1024
SEQ = 2048
HEAD_DIM = 64
DIL_PATTERNS = ((128, 1), (512, 4), (2048, 16))
DILATIONS = tuple(d for _, d in DIL_PATTERNS)
N_GROUPS_A = 3
HEADS_PER_GROUP_A = 6
N_HEADS_A = N_GROUPS_A * HEADS_PER_GROUP_A
WIDTH_A = N_HEADS_A * HEAD_DIM
GROUP_W = HEADS_PER_GROUP_A * HEAD_DIM
N_HEADS_B = 14
WIDTH_B = N_HEADS_B * HEAD_DIM
GRID_W = 64
GRID_ROWS = SEQ // GRID_W
NA_KH = 8
NA_KW = 16
RMS_EPS = 1e-6
NEG_INF = -1e30
LOG2E = math.log2(math.e)

LANES = 128
N_SLABS = D_MODEL // LANES
PAIR_W = 2 * HEAD_DIM
PAIRS_A = HEADS_PER_GROUP_A // 2
PAIRS_B = N_HEADS_B // 2
HALF_WIN = 64
QBLK_A = 128
KWIN_A = QBLK_A + 2 * HALF_WIN
NBR_KEYS = NA_KH * GRID_W
SOFTMAX_CHUNK = 32 * 1024
NBR_ROWS_PER_ITER = 16
DIL_UNITS_PER_STEP = 16
VMEM_LIMIT = 56 * 1024 * 1024

QKV_A_W = 3 * GROUP_W
SEG_A_W = 4 * GROUP_W
SEG_B_W = 4 * WIDTH_B
GATE_W = 2 * D_MODEL
INPROJ_SEGMENTS = tuple((d, SEG_A_W) for d in DILATIONS) + ((1, SEG_B_W), (1, GATE_W))
PROJ_CHUNK = 512
TM_IN = 512
TM_OUT = 512
MERGE_SPLIT = 1


def _permute_w_in(w_in, layer):
    scale = LOG2E * HEAD_DIM ** -0.5
    qa, ka, va, za = (w_in[layer, :, i * WIDTH_A:(i + 1) * WIDTH_A] for i in range(4))
    ob = 4 * WIDTH_A
    qb, kb, vb, zb = (w_in[layer, :, ob + i * WIDTH_B: ob + (i + 1) * WIDTH_B] for i in range(4))
    gates = w_in[layer, :, ob + 4 * WIDTH_B:]
    cols = []
    for g in range(N_GROUPS_A):
        sl = slice(g * GROUP_W, (g + 1) * GROUP_W)
        cols += [qa[:, sl] * scale, ka[:, sl], va[:, sl], za[:, sl]]
    cols += [qb * scale, kb, vb, zb, gates]
    return jnp.concatenate(cols, axis=1).astype(jnp.bfloat16)


def _inproj_kernel(*refs):
    n_seg = len(INPROJ_SEGMENTS)
    x_refs = refs[:N_SLABS]
    xf_ref, g_ref, w_ref = refs[N_SLABS:N_SLABS + 3]
    out_refs = refs[N_SLABS + 3:N_SLABS + 3 + n_seg]
    hn_refs = dict(zip(DILATIONS[1:], refs[N_SLABS + 3 + n_seg:]))

    def normalise(slabs, hn_ref, rows):
        ssq = sum(jnp.sum(s * s, axis=-1, keepdims=True) for s in slabs)
        rstd = lax.rsqrt(ssq * (1.0 / D_MODEL) + RMS_EPS)
        for c, s in enumerate(slabs):
            cols = slice(c * LANES, (c + 1) * LANES)
            hn_ref[rows, cols] = (s * rstd * g_ref[:, cols]).astype(hn_ref.dtype)

    xf = xf_ref[0]
    hn_nat = (xf * lax.rsqrt(jnp.mean(xf * xf, axis=-1, keepdims=True) + RMS_EPS) * g_ref[...]).astype(jnp.bfloat16)
    for d in DILATIONS[1:]:
        n = TM_IN // d
        for r in range(d):
            normalise([x_ref[0, pl.ds(r, n, stride=d), :] for x_ref in x_refs], hn_refs[d],
                      slice(r * n, (r + 1) * n))
    col = 0
    for o_ref, (d, width) in zip(out_refs, INPROJ_SEGMENTS):
        n = TM_IN // d
        for a in range(0, width, PROJ_CHUNK):
            res = jnp.dot(hn_nat if d == 1 else hn_refs[d][...], w_ref[:, col + a:col + a + PROJ_CHUNK],
                          preferred_element_type=jnp.float32).astype(jnp.bfloat16)
            if d == 1:
                o_ref[0, :, a:a + PROJ_CHUNK] = res
            else:
                for r in range(d):
                    o_ref[0, r, :, a:a + PROJ_CHUNK] = res[r * n:(r + 1) * n]
        col += width


def _inproj_call(x, norm_pre, w_perm):
    b, t, _ = x.shape
    bf16 = jnp.bfloat16
    out_shape, out_specs = [], []
    for d, width in INPROJ_SEGMENTS:
        if d == 1:
            out_shape.append(jax.ShapeDtypeStruct((b, t, width), bf16))
            out_specs.append(pl.BlockSpec((1, TM_IN, width), lambda i, j: (i, j, 0)))
        else:
            out_shape.append(jax.ShapeDtypeStruct((b, d, t // d, width), bf16))
            out_specs.append(pl.BlockSpec((1, d, TM_IN // d, width), lambda i, j: (i, 0, j, 0)))
    x_specs = [pl.BlockSpec((1, TM_IN, LANES), functools.partial(lambda i, j, c: (i, j, c), c=c))
               for c in range(N_SLABS)]
    return pl.pallas_call(
        _inproj_kernel,
        grid=(b, t // TM_IN),
        in_specs=x_specs + [
            pl.BlockSpec((1, TM_IN, D_MODEL), lambda i, j: (i, j, 0)),
            pl.BlockSpec((1, D_MODEL), lambda i, j: (0, 0)),
            pl.BlockSpec(w_perm.shape, lambda i, j: (0, 0), pipeline_mode=pl.Buffered(1)),
        ],
        out_specs=out_specs,
        out_shape=out_shape,
        scratch_shapes=[pltpu.VMEM((TM_IN, D_MODEL), bf16)] * (len(DILATIONS) - 1),
        compiler_params=pltpu.CompilerParams(
            dimension_semantics=("arbitrary", "arbitrary"), vmem_limit_bytes=VMEM_LIMIT),
        name="inproj",
    )(*([x] * N_SLABS), x, norm_pre.reshape(1, D_MODEL), w_perm)


def _stack_heads(q2):
    lane = lax.broadcasted_iota(jnp.int32, q2.shape, 1)
    zero = jnp.zeros_like(q2)
    return jnp.concatenate([jnp.where(lane < HEAD_DIM, q2, zero), jnp.where(lane < HEAD_DIM, zero, q2)], axis=0)


def _score_dot(qs, k2):
    return lax.dot_general(qs, k2, (((1,), (1,)), ((), ())), preferred_element_type=jnp.float32)


def _softmax_rows(s_ref, bias_ref, p_ref, n_rows):
    chunk = SOFTMAX_CHUNK // s_ref.shape[-1]
    maxima = []
    for r0 in range(0, n_rows, chunk):
        rows = slice(r0, r0 + chunk)
        s = s_ref[rows, :] + bias_ref[rows, :]
        m = jnp.max(s, axis=-1, keepdims=True)
        p_ref[rows, :] = jnp.exp2(s - m).astype(p_ref.dtype)
        maxima.append(jnp.broadcast_to(m, (chunk, LANES)))
    return jnp.concatenate(maxima, axis=0)


def _values_with_ones(p, v2):
    vx = jnp.concatenate([v2, jnp.ones_like(v2)], axis=1)
    pvx = jnp.dot(p, vx, preferred_element_type=jnp.float32)
    return pvx[:, :LANES], pvx[:, LANES:]


def _alibi_slopes():
    return (2.0 ** (-8.0 * np.arange(1, N_HEADS_A + 1) / N_HEADS_A)).astype(np.float32)


def _dilated_bias_table(g, seq_len):
    d = DILATIONS[g]
    sub_len = seq_len // d
    kwin = min(KWIN_A, sub_len)
    deltas = (-HALF_WIN, 0, QBLK_A - kwin) if sub_len > QBLK_A else (0,)
    slopes = _alibi_slopes()[g::N_GROUPS_A]
    i = np.arange(QBLK_A)[:, None]
    j = np.arange(kwin)[None, :]
    tab = np.empty((PAIRS_A, len(deltas), 2 * QBLK_A, kwin), np.float32)
    for vi, delta in enumerate(deltas):
        rel = np.abs(j - i + delta)
        for p in range(PAIRS_A):
            for hh in range(2):
                bias = np.where(rel <= HALF_WIN,
                                np.float32(LOG2E) * (-slopes[2 * p + hh] * (rel * d).astype(np.float32)), NEG_INF)
                tab[p, vi, hh * QBLK_A:(hh + 1) * QBLK_A] = bias
    return jnp.asarray(tab)


def _dilated_kernel(qkv_ref, bias_ref, o_ref, st_ref, s_buf, p_buf, m_scr, l_scr, *, sub_len, n_res):
    kwin = min(KWIN_A, sub_len)
    n_blk = sub_len // QBLK_A
    n_units = n_blk * n_res
    lane = lax.broadcasted_iota(jnp.int32, (QBLK_A, LANES), 1)

    def unit(j):
        res, n = divmod(j, n_blk)
        if n_blk == 1:
            return res, 0, 0, 0
        k0 = min(max(n * QBLK_A - HALF_WIN, 0), sub_len - kwin)
        return res, (1 if n == 0 else 2 if n == n_blk - 1 else 0), n * QBLK_A, k0

    def scores(j, p, slot):
        res, _, q0, k0 = unit(j)
        c = p * PAIR_W
        qs = _stack_heads(qkv_ref[0, res, pl.ds(q0, QBLK_A), c:c + PAIR_W])
        s_buf[slot] = _score_dot(qs, qkv_ref[0, res, pl.ds(k0, kwin), c + GROUP_W:c + GROUP_W + PAIR_W])

    def head_lanes(tile_ref, j, p, first_lane, col):
        tile = jnp.zeros((QBLK_A, LANES), jnp.float32) if p == 0 else tile_ref[j % 2]
        tile = jnp.where(lane == first_lane + 2 * p, col[:QBLK_A], tile)
        return jnp.where(lane == first_lane + 2 * p + 1, col[QBLK_A:], tile)

    def softmax(j, p, slot):
        _, var, _, _ = unit(j)
        m = _softmax_rows(s_buf.at[slot], bias_ref.at[p, var], p_buf.at[slot], 2 * QBLK_A)
        m_scr[j % 2] = head_lanes(m_scr, j, p, 0, m)

    def weighted_values(j, p, slot):
        res, _, q0, k0 = unit(j)
        c = p * PAIR_W + 2 * GROUP_W
        pv, l = _values_with_ones(p_buf[slot], qkv_ref[0, res, pl.ds(k0, kwin), c:c + PAIR_W])
        o_ref[0, res, pl.ds(q0, QBLK_A), p * PAIR_W:(p + 1) * PAIR_W] = (
            jnp.where(lane < HEAD_DIM, pv[:QBLK_A], pv[QBLK_A:]).astype(o_ref.dtype))
        sums = head_lanes(l_scr, j, p, HEADS_PER_GROUP_A, l)
        if p == PAIRS_A - 1:
            st_ref[0, res, pl.ds(q0, QBLK_A), :] = jnp.where(lane < HEADS_PER_GROUP_A, m_scr[j % 2], sums)
        else:
            l_scr[j % 2] = sums

    items = [(j, p) for j in range(n_units) for p in range(PAIRS_A)]
    scores(*items[0], 0)
    for i, item in enumerate(items):
        slot = i % 2
        if i + 1 < len(items):
            scores(*items[i + 1], 1 - slot)
        if i > 0:
            weighted_values(*items[i - 1], 1 - slot)
        softmax(*item, slot)
    weighted_values(*items[-1], (len(items) - 1) % 2)


def _dilated_call(seg, g):
    b, d, sub_len, _ = seg.shape
    n_res = min(d, DIL_UNITS_PER_STEP * QBLK_A // sub_len)
    kwin = min(KWIN_A, sub_len)
    bias = _dilated_bias_table(g, d * sub_len)
    return pl.pallas_call(
        functools.partial(_dilated_kernel, sub_len=sub_len, n_res=n_res),
        grid=(b, d // n_res),
        in_specs=[
            pl.BlockSpec((1, n_res, sub_len, QKV_A_W), lambda i, j: (i, j, 0, 0)),
            pl.BlockSpec(bias.shape, lambda i, j: (0, 0, 0, 0)),
        ],
        out_specs=[
            pl.BlockSpec((1, n_res, sub_len, GROUP_W), lambda i, j: (i, j, 0, 0)),
            pl.BlockSpec((1, n_res, sub_len, LANES), lambda i, j: (i, j, 0, 0)),
        ],
        out_shape=[
            jax.ShapeDtypeStruct((b, d, sub_len, GROUP_W), jnp.bfloat16),
            jax.ShapeDtypeStruct((b, d, sub_len, LANES), jnp.float32),
        ],
        scratch_shapes=[pltpu.VMEM((2, 2 * QBLK_A, kwin), jnp.float32),
                        pltpu.VMEM((2, 2 * QBLK_A, kwin), jnp.bfloat16),
                        pltpu.VMEM((2, QBLK_A, LANES), jnp.float32),
                        pltpu.VMEM((2, QBLK_A, LANES), jnp.float32)],
        compiler_params=pltpu.CompilerParams(
            dimension_semantics=("arbitrary", "arbitrary"), vmem_limit_bytes=VMEM_LIMIT),
        name=f"dilated_g{g}",
    )(seg, bias)


def _nbr_bias_table(rpb):
    rpb = rpb.astype(jnp.float32) * LOG2E
    n_dc = 2 * NA_KW - 1
    pad = GRID_W - NA_KW
    ext = jnp.concatenate([jnp.repeat(rpb[..., :1], pad, axis=-1), rpb,
                           jnp.repeat(rpb[..., n_dc - 1:], pad, axis=-1)], axis=-1)
    toep = jnp.stack([ext[..., GRID_W - 1 - q: 2 * GRID_W - 1 - q] for q in range(GRID_W)], axis=-2)
    qc = np.arange(GRID_W)[:, None]
    kc = np.arange(GRID_W)[None, :]
    cs = np.clip(qc - NA_KW // 2, 0, GRID_W - NA_KW)
    valid = (kc >= cs) & (kc < cs + NA_KW)
    toep = jnp.where(valid, toep, NEG_INF)
    tab = jnp.stack([toep[:, v:v + NA_KH] for v in range(NA_KH)], axis=1)
    tab = tab.reshape(PAIRS_B, 2, NA_KH, NA_KH, GRID_W, GRID_W).transpose(0, 2, 1, 4, 3, 5)
    return tab.reshape(PAIRS_B, NA_KH, 2 * GRID_W, NBR_KEYS)


def _nbr_kernel(q_ref, k_ref, v_ref, bias_ref, o_ref, s_buf, p_buf):
    lane = lax.broadcasted_iota(jnp.int32, (GRID_W, LANES), 1)

    def rows(r):
        if isinstance(r, int):
            rs = min(max(r - NA_KH // 2, 0), GRID_ROWS - NA_KH)
            return rs - r + NA_KH - 1, r * GRID_W, rs * GRID_W
        rs = jnp.clip(r - NA_KH // 2, 0, GRID_ROWS - NA_KH)
        return (rs - r + NA_KH - 1, pl.multiple_of(r * GRID_W, GRID_W), pl.multiple_of(rs * GRID_W, GRID_W))

    def scores(r, slot):
        _, q0, k0 = rows(r)
        s_buf[slot] = _score_dot(_stack_heads(q_ref[0, pl.ds(q0, GRID_W), :]), k_ref[0, pl.ds(k0, NBR_KEYS), :])

    def softmax(r, slot):
        var, _, _ = rows(r)
        _softmax_rows(s_buf.at[slot], bias_ref.at[0, var], p_buf.at[slot], 2 * GRID_W)

    def weighted_values(r, slot):
        _, q0, k0 = rows(r)
        pv, l = _values_with_ones(p_buf[slot], v_ref[0, pl.ds(k0, NBR_KEYS), :])
        pvn = pv * (1.0 / l)
        o_ref[0, pl.ds(q0, GRID_W), :] = jnp.where(lane < HEAD_DIM, pvn[:GRID_W], pvn[GRID_W:]).astype(o_ref.dtype)

    def row_group(j, first=False, last=False):
        for u in range(NBR_ROWS_PER_ITER):
            r, slot = j * NBR_ROWS_PER_ITER + u, u % 2
            if not (last and u == NBR_ROWS_PER_ITER - 1):
                scores(r + 1, 1 - slot)
            if not (first and u == 0):
                weighted_values(r - 1, 1 - slot)
            softmax(r, slot)

    n_iter = GRID_ROWS // NBR_ROWS_PER_ITER
    scores(0, 0)
    row_group(0, first=True)

    def body(j, carry):
        row_group(j)
        return carry

    if n_iter > 2:
        lax.fori_loop(1, n_iter - 1, body, 0)
    row_group(n_iter - 1, last=True)
    weighted_values(GRID_ROWS - 1, 1)


def _nbr_call(seg, bias):
    b, t, _ = seg.shape
    return pl.pallas_call(
        _nbr_kernel,
        grid=(PAIRS_B, b),
        in_specs=[
            pl.BlockSpec((1, t, PAIR_W), lambda p, i: (i, 0, p)),
            pl.BlockSpec((1, t, PAIR_W), lambda p, i: (i, 0, PAIRS_B + p)),
            pl.BlockSpec((1, t, PAIR_W), lambda p, i: (i, 0, 2 * PAIRS_B + p)),
            pl.BlockSpec((1, NA_KH, 2 * GRID_W, NBR_KEYS), lambda p, i: (p, 0, 0, 0)),
        ],
        out_specs=pl.BlockSpec((1, t, PAIR_W), lambda p, i: (i, 0, p)),
        out_shape=jax.ShapeDtypeStruct((b, t, WIDTH_B), jnp.bfloat16),
        scratch_shapes=[pltpu.VMEM((2, 2 * GRID_W, NBR_KEYS), jnp.float32),
                        pltpu.VMEM((2, 2 * GRID_W, NBR_KEYS), jnp.bfloat16)],
        compiler_params=pltpu.CompilerParams(
            dimension_semantics=("arbitrary", "arbitrary"), vmem_limit_bytes=VMEM_LIMIT),
        name="nbr",
    )(seg, seg, seg, bias)


def _silu(z):
    return z * jax.nn.sigmoid(z)


def _merge_kernel(x_ref, o0_ref, o1_ref, o2_ref, st0_ref, st1_ref, st2_ref, z0_ref, z1_ref, z2_ref, ob_ref, zb_ref,
                  g_ref, expand_ref, wpa_ref, wpb_ref, wout_ref, bg_ref, npost_ref, y_ref, stats_nat, gated_nat):
    f32, bf16 = jnp.float32, jnp.bfloat16
    hm = x_ref.shape[1] // MERGE_SPLIT
    lane = lax.broadcasted_iota(jnp.int32, (hm, LANES), 1)
    n_slabs = GROUP_W // LANES

    for h in range(MERGE_SPLIT):
        rows = slice(h * hm, (h + 1) * hm)
        ub = (ob_ref[0, rows, :].astype(f32) * _silu(zb_ref[0, rows, :].astype(f32))).astype(bf16)
        yb = jnp.dot(ub, wpb_ref[...], preferred_element_type=f32)

        stats = [st0_ref[0, rows, :]]
        gated = [o0_ref[0, rows, :].astype(f32) * _silu(z0_ref[0, rows, :].astype(f32))]
        for gi, (d, o_ref, st_ref, z_ref) in enumerate(zip(DILATIONS[1:], (o1_ref, o2_ref), (st1_ref, st2_ref),
                                                             (z1_ref, z2_ref))):
            n = hm // d
            sub = slice(h * n, (h + 1) * n)
            for r in range(d):
                nat_rows = pl.ds(r, n, stride=d)
                stats_nat[h, gi, nat_rows, :] = st_ref[0, r, sub, :]
                w = o_ref[0, r, sub, :].astype(f32) * _silu(z_ref[0, r, sub, :].astype(f32))
                for c in range(n_slabs):
                    gated_nat[h, gi, c, nat_rows, :] = w[:, c * LANES:(c + 1) * LANES]
            stats.append(stats_nat[h, gi])
            gated.append(jnp.concatenate([gated_nat[h, gi, c] for c in range(n_slabs)], axis=1))
        sums = [pltpu.roll(st, LANES - HEADS_PER_GROUP_A, 1) for st in stats]
        mx = jnp.maximum(jnp.maximum(stats[0], stats[1]), stats[2])
        es = [jnp.exp2(st - mx) for st in stats]
        inv = 1.0 / (sums[0] * es[0] + sums[1] * es[1] + sums[2] * es[2])
        weights = [jnp.where(lane < HEADS_PER_GROUP_A, e * inv, 0.0) for e in es]
        comb = (weights[0] + pltpu.roll(weights[1], HEADS_PER_GROUP_A, 1)
                + pltpu.roll(weights[2], 2 * HEADS_PER_GROUP_A, 1))
        hi = comb.astype(bf16)
        lo = (comb - hi.astype(f32)).astype(bf16)
        wexp = jnp.dot(jnp.concatenate([hi, lo], axis=1), expand_ref[...], preferred_element_type=f32)
        ua = (jnp.concatenate(gated, axis=1) * wexp).astype(bf16)
        ya = jnp.dot(ua, wpa_ref[...], preferred_element_type=f32)
        ga = g_ref[0, rows, :D_MODEL].astype(f32) + bg_ref[0:1, :]
        gb = g_ref[0, rows, D_MODEL:].astype(f32) + bg_ref[1:2, :]
        merged = jax.nn.sigmoid(ga) * ya + jax.nn.sigmoid(gb) * yb
        out = jnp.dot(merged.astype(bf16), wout_ref[...], preferred_element_type=f32)
        ms = jnp.mean(out * out, axis=-1, keepdims=True)
        y_ref[0, rows, :] = x_ref[0, rows, :] + out * lax.rsqrt(ms + RMS_EPS) * npost_ref[...]


def _merge_call(x, segs_a, oas, stats, ob, seg_b, gates, b_gate, w_proj_a, w_proj_b, w_out, norm_post):
    b, t, _ = x.shape
    expand = np.zeros((2 * LANES, WIDTH_A), np.float32)
    for h in range(N_HEADS_A):
        expand[h, h * HEAD_DIM:(h + 1) * HEAD_DIM] = 1.0
        expand[LANES + h, h * HEAD_DIM:(h + 1) * HEAD_DIM] = 1.0
    z_col = QKV_A_W // GROUP_W

    def tok(w, col=0):
        return pl.BlockSpec((1, TM_OUT, w), lambda i, j: (i, j, col))

    def res_major(d, w, col=0):
        return pl.BlockSpec((1, d, TM_OUT // d, w), lambda i, j: (i, 0, j, col))

    full = lambda shp: pl.BlockSpec(shp, lambda i, j: (0,) * len(shp))
    d1, d2 = DILATIONS[1:]
    return pl.pallas_call(
        _merge_kernel,
        grid=(b, t // TM_OUT),
        in_specs=[tok(D_MODEL),
                  tok(GROUP_W), res_major(d1, GROUP_W), res_major(d2, GROUP_W),
                  tok(LANES), res_major(d1, LANES), res_major(d2, LANES),
                  tok(GROUP_W, z_col), res_major(d1, GROUP_W, z_col), res_major(d2, GROUP_W, z_col),
                  tok(WIDTH_B), tok(WIDTH_B, 3), tok(GATE_W),
                  full((2 * LANES, WIDTH_A)), full((WIDTH_A, D_MODEL)), full((WIDTH_B, D_MODEL)),
                  full((D_MODEL, D_MODEL)), full((2, D_MODEL)), full((1, D_MODEL))],
        out_specs=tok(D_MODEL),
        out_shape=jax.ShapeDtypeStruct((b, t, D_MODEL), x.dtype),
        scratch_shapes=[pltpu.VMEM((MERGE_SPLIT, 2, TM_OUT // MERGE_SPLIT, LANES), jnp.float32),
                        pltpu.VMEM((MERGE_SPLIT, 2, GROUP_W // LANES, TM_OUT // MERGE_SPLIT, LANES), jnp.float32)],
        compiler_params=pltpu.CompilerParams(
            dimension_semantics=("arbitrary", "arbitrary"), vmem_limit_bytes=VMEM_LIMIT),
        name="merge",
    )(x, *oas, *stats, *segs_a, ob, seg_b, gates, jnp.asarray(expand, jnp.bfloat16),
      w_proj_a.astype(jnp.bfloat16), w_proj_b.astype(jnp.bfloat16), w_out.astype(jnp.bfloat16),
      b_gate, norm_post.reshape(1, D_MODEL))


def _encoder_layer(x, norm_pre, w_perm, b_gate, nbr_bias, w_proj_a, w_proj_b, w_out, norm_post):
    b, t, _ = x.shape
    *segs_a, seg_b, gates = _inproj_call(x, norm_pre, w_perm)
    att = [_dilated_call(seg.reshape(b, DILATIONS[g], t // DILATIONS[g], SEG_A_W), g)
           for g, seg in enumerate(segs_a)]
    oas = [att[0][0].reshape(b, t, GROUP_W), att[1][0], att[2][0]]
    stats = [att[0][1].reshape(b, t, LANES), att[1][1], att[2][1]]
    ob = _nbr_call(seg_b, nbr_bias)
    return _merge_call(x, segs_a, oas, stats, ob, seg_b, gates, b_gate, w_proj_a, w_proj_b, w_out, norm_post)


def kernel(x_prompt, x_sample, norm_pre, w_in, b_gate, rpb, w_proj_a, w_proj_b, w_out, norm_post):
    y_prompt, y_sample = x_prompt, x_sample
    for l in range(norm_pre.shape[0]):
        w_perm = _permute_w_in(w_in, l)
        nbr_bias = _nbr_bias_table(rpb[l])
        args = (norm_pre[l], w_perm, b_gate[l], nbr_bias, w_proj_a[l], w_proj_b[l], w_out[l], norm_post[l])
        y_prompt = _encoder_layer(y_prompt, *args)
        y_sample = _encoder_layer(y_sample, *args)
    return (y_prompt, y_sample)
```

```python
import functools
import math

import numpy as np
import jax
import jax.numpy as jnp
from jax import lax
from jax.experimental import pallas as pl
from jax.experimental.pallas import tpu as pltpu

D_MODEL = 1024
SEQ = 2048
HEAD_DIM = 64
DIL_PATTERNS = ((128, 1), (512, 4), (2048, 16))
DILATIONS = tuple(d for _, d in DIL_PATTERNS)
N_GROUPS_A = 3
HEADS_PER_GROUP_A = 6
N_HEADS_A = N_GROUPS_A * HEADS_PER_GROUP_A
WIDTH_A = N_HEADS_A * HEAD_DIM
GROUP_W = HEADS_PER_GROUP_A * HEAD_DIM
N_HEADS_B = 14
WIDTH_B = N_HEADS_B * HEAD_DIM
GRID_W = 64
GRID_ROWS = SEQ // GRID_W
NA_KH = 8
NA_KW = 16
RMS_EPS = 1e-6
NEG_INF = -1e30
LOG2E = math.log2(math.e)

LANES = 128
N_SLABS = D_MODEL // LANES
PAIR_W = 2 * HEAD_DIM
PAIRS_A = HEADS_PER_GROUP_A // 2
PAIRS_B = N_HEADS_B // 2
HALF_WIN = 64
QBLK_A = 128
KWIN_A = QBLK_A + 2 * HALF_WIN
NBR_KEYS = NA_KH * GRID_W
NBR_BIAS_ROWS = 2 * NA_KH - 2
SOFTMAX_CHUNK = 32 * 1024
NBR_ROWS_PER_ITER = 16
DIL_UNITS_PER_STEP = 16
VMEM_LIMIT = 56 * 1024 * 1024

QKV_A_W = 3 * GROUP_W
SEG_A_W = 4 * GROUP_W
SEG_B_W = 4 * WIDTH_B
GATE_W = 2 * D_MODEL
INPROJ_SEGMENTS = tuple((d, SEG_A_W) for d in DILATIONS) + ((1, SEG_B_W), (1, GATE_W))
PROJ_CHUNK = 512
TM_IN = 512
TM_OUT = 512
MERGE_SPLIT = 1


def _permute_w_in(w_in, layer):
    scale = LOG2E * HEAD_DIM ** -0.5
    qa, ka, va, za = (w_in[layer, :, i * WIDTH_A:(i + 1) * WIDTH_A] for i in range(4))
    ob = 4 * WIDTH_A
    qb, kb, vb, zb = (w_in[layer, :, ob + i * WIDTH_B: ob + (i + 1) * WIDTH_B] for i in range(4))
    gates = w_in[layer, :, ob + 4 * WIDTH_B:]
    cols = []
    for g in range(N_GROUPS_A):
        sl = slice(g * GROUP_W, (g + 1) * GROUP_W)
        cols += [qa[:, sl] * scale, ka[:, sl], va[:, sl], za[:, sl]]
    cols += [qb * scale, kb, vb, zb, gates]
    return jnp.concatenate(cols, axis=1).astype(jnp.bfloat16)


def _inproj_kernel(*refs):
    n_seg = len(INPROJ_SEGMENTS)
    x_refs = refs[:N_SLABS]
    xf_ref, g_ref, w_ref = refs[N_SLABS:N_SLABS + 3]
    out_refs = refs[N_SLABS + 3:N_SLABS + 3 + n_seg]
    hn_refs = dict(zip(DILATIONS[1:], refs[N_SLABS + 3 + n_seg:]))

    def normalise(slabs, hn_ref, rows):
        ssq = sum(jnp.sum(s * s, axis=-1, keepdims=True) for s in slabs)
        rstd = lax.rsqrt(ssq * (1.0 / D_MODEL) + RMS_EPS)
        for c, s in enumerate(slabs):
            cols = slice(c * LANES, (c + 1) * LANES)
            hn_ref[rows, cols] = (s * rstd * g_ref[:, cols]).astype(hn_ref.dtype)

    xf = xf_ref[0]
    hn_nat = (xf * lax.rsqrt(jnp.mean(xf * xf, axis=-1, keepdims=True) + RMS_EPS) * g_ref[...]).astype(jnp.bfloat16)
    for d in DILATIONS[1:]:
        n = TM_IN // d
        for r in range(d):
            normalise([x_ref[0, pl.ds(r, n, stride=d), :] for x_ref in x_refs], hn_refs[d],
                      slice(r * n, (r + 1) * n))
    col = 0
    for o_ref, (d, width) in zip(out_refs, INPROJ_SEGMENTS):
        n = TM_IN // d
        for a in range(0, width, PROJ_CHUNK):
            res = jnp.dot(hn_nat if d == 1 else hn_refs[d][...], w_ref[:, col + a:col + a + PROJ_CHUNK],
                          preferred_element_type=jnp.float32).astype(jnp.bfloat16)
            if d == 1:
                o_ref[0, :, a:a + PROJ_CHUNK] = res
            else:
                for r in range(d):
                    o_ref[0, r, :, a:a + PROJ_CHUNK] = res[r * n:(r + 1) * n]
        col += width


def _inproj_call(x, norm_pre, w_perm):
    b, t, _ = x.shape
    bf16 = jnp.bfloat16
    out_shape, out_specs = [], []
    for d, width in INPROJ_SEGMENTS:
        if d == 1:
            out_shape.append(jax.ShapeDtypeStruct((b, t, width), bf16))
            out_specs.append(pl.BlockSpec((1, TM_IN, width), lambda i, j: (i, j, 0)))
        else:
            out_shape.append(jax.ShapeDtypeStruct((b, d, t // d, width), bf16))
            out_specs.append(pl.BlockSpec((1, d, TM_IN // d, width), lambda i, j: (i, 0, j, 0)))
    x_specs = [pl.BlockSpec((1, TM_IN, LANES), functools.partial(lambda i, j, c: (i, j, c), c=c))
               for c in range(N_SLABS)]
    return pl.pallas_call(
        _inproj_kernel,
        grid=(b, t // TM_IN),
        in_specs=x_specs + [
            pl.BlockSpec((1, TM_IN, D_MODEL), lambda i, j: (i, j, 0)),
            pl.BlockSpec((1, D_MODEL), lambda i, j: (0, 0)),
            pl.BlockSpec(w_perm.shape, lambda i, j: (0, 0), pipeline_mode=pl.Buffered(1)),
        ],
        out_specs=out_specs,
        out_shape=out_shape,
        scratch_shapes=[pltpu.VMEM((TM_IN, D_MODEL), bf16)] * (len(DILATIONS) - 1),
        compiler_params=pltpu.CompilerParams(
            dimension_semantics=("arbitrary", "arbitrary"), vmem_limit_bytes=VMEM_LIMIT),
        name="inproj",
    )(*([x] * N_SLABS), x, norm_pre.reshape(1, D_MODEL), w_perm)


def _stack_heads(q2):
    lane = lax.broadcasted_iota(jnp.int32, q2.shape, 1)
    zero = jnp.zeros_like(q2)
    return jnp.concatenate([jnp.where(lane < HEAD_DIM, q2, zero), jnp.where(lane < HEAD_DIM, zero, q2)], axis=0)


def _score_dot(qs, k2):
    return lax.dot_general(qs, k2, (((1,), (1,)), ((), ())), preferred_element_type=jnp.float32)


def _softmax_rows(s_ref, bias_rows, p_ref, n_rows):
    chunk = SOFTMAX_CHUNK // s_ref.shape[-1]
    maxima = []
    for r0 in range(0, n_rows, chunk):
        rows = slice(r0, r0 + chunk)
        s = s_ref[rows, :] + bias_rows(rows)
        m = jnp.max(s, axis=-1, keepdims=True)
        p_ref[rows, :] = jnp.exp2(s - m).astype(p_ref.dtype)
        maxima.append(jnp.broadcast_to(m, (chunk, LANES)))
    return jnp.concatenate(maxima, axis=0)


def _values_with_ones(p, v2):
    vx = jnp.concatenate([v2, jnp.ones_like(v2)], axis=1)
    pvx = jnp.dot(p, vx, preferred_element_type=jnp.float32)
    return pvx[:, :LANES], pvx[:, LANES:]


def _alibi_slopes():
    return (2.0 ** (-8.0 * np.arange(1, N_HEADS_A + 1) / N_HEADS_A)).astype(np.float32)


def _dilated_bias_table(g, seq_len):
    d = DILATIONS[g]
    sub_len = seq_len // d
    kwin = min(KWIN_A, sub_len)
    deltas = (-HALF_WIN, 0, QBLK_A - kwin) if sub_len > QBLK_A else (0,)
    slopes = _alibi_slopes()[g::N_GROUPS_A]
    i = np.arange(QBLK_A)[:, None]
    j = np.arange(kwin)[None, :]
    tab = np.empty((PAIRS_A, len(deltas), 2 * QBLK_A, kwin), np.float32)
    for vi, delta in enumerate(deltas):
        rel = np.abs(j - i + delta)
        for p in range(PAIRS_A):
            for hh in range(2):
                bias = np.where(rel <= HALF_WIN,
                                np.float32(LOG2E) * (-slopes[2 * p + hh] * (rel * d).astype(np.float32)), NEG_INF)
                tab[p, vi, hh * QBLK_A:(hh + 1) * QBLK_A] = bias
    return jnp.asarray(tab)


def _dilated_kernel(qkv_ref, bias_ref, o_ref, st_ref, s_buf, p_buf, m_scr, l_scr, *, sub_len, n_res):
    kwin = min(KWIN_A, sub_len)
    n_blk = sub_len // QBLK_A
    n_units = n_blk * n_res
    lane = lax.broadcasted_iota(jnp.int32, (QBLK_A, LANES), 1)

    def unit(j):
        res, n = divmod(j, n_blk)
        if n_blk == 1:
            return res, 0, 0, 0
        k0 = min(max(n * QBLK_A - HALF_WIN, 0), sub_len - kwin)
        return res, (1 if n == 0 else 2 if n == n_blk - 1 else 0), n * QBLK_A, k0

    def scores(j, p, slot):
        res, _, q0, k0 = unit(j)
        c = p * PAIR_W
        qs = _stack_heads(qkv_ref[0, res, pl.ds(q0, QBLK_A), c:c + PAIR_W])
        s_buf[slot] = _score_dot(qs, qkv_ref[0, res, pl.ds(k0, kwin), c + GROUP_W:c + GROUP_W + PAIR_W])

    def head_lanes(tile_ref, j, p, first_lane, col):
        tile = jnp.zeros((QBLK_A, LANES), jnp.float32) if p == 0 else tile_ref[j % 2]
        tile = jnp.where(lane == first_lane + 2 * p, col[:QBLK_A], tile)
        return jnp.where(lane == first_lane + 2 * p + 1, col[QBLK_A:], tile)

    def softmax(j, p, slot):
        _, var, _, _ = unit(j)
        m = _softmax_rows(s_buf.at[slot], lambda rows: bias_ref[p, var, rows, :], p_buf.at[slot], 2 * QBLK_A)
        m_scr[j % 2] = head_lanes(m_scr, j, p, 0, m)

    def weighted_values(j, p, slot):
        res, _, q0, k0 = unit(j)
        c = p * PAIR_W + 2 * GROUP_W
        pv, l = _values_with_ones(p_buf[slot], qkv_ref[0, res, pl.ds(k0, kwin), c:c + PAIR_W])
        o_ref[0, res, pl.ds(q0, QBLK_A), p * PAIR_W:(p + 1) * PAIR_W] = (
            jnp.where(lane < HEAD_DIM, pv[:QBLK_A], pv[QBLK_A:]).astype(o_ref.dtype))
        sums = head_lanes(l_scr, j, p, HEADS_PER_GROUP_A, l)
        if p == PAIRS_A - 1:
            st_ref[0, res, pl.ds(q0, QBLK_A), :] = jnp.where(lane < HEADS_PER_GROUP_A, m_scr[j % 2], sums)
        else:
            l_scr[j % 2] = sums

    items = [(j, p) for j in range(n_units) for p in range(PAIRS_A)]
    scores(*items[0], 0)
    for i, item in enumerate(items):
        slot = i % 2
        if i + 1 < len(items):
            scores(*items[i + 1], 1 - slot)
        if i > 0:
            weighted_values(*items[i - 1], 1 - slot)
        softmax(*item, slot)
    weighted_values(*items[-1], (len(items) - 1) % 2)


def _dilated_call(seg, g):
    b, d, sub_len, _ = seg.shape
    n_res = min(d, DIL_UNITS_PER_STEP * QBLK_A // sub_len)
    kwin = min(KWIN_A, sub_len)
    bias = _dilated_bias_table(g, d * sub_len)
    return pl.pallas_call(
        functools.partial(_dilated_kernel, sub_len=sub_len, n_res=n_res),
        grid=(b, d // n_res),
        in_specs=[
            pl.BlockSpec((1, n_res, sub_len, QKV_A_W), lambda i, j: (i, j, 0, 0)),
            pl.BlockSpec(bias.shape, lambda i, j: (0, 0, 0, 0)),
        ],
        out_specs=[
            pl.BlockSpec((1, n_res, sub_len, GROUP_W), lambda i, j: (i, j, 0, 0)),
            pl.BlockSpec((1, n_res, sub_len, LANES), lambda i, j: (i, j, 0, 0)),
        ],
        out_shape=[
            jax.ShapeDtypeStruct((b, d, sub_len, GROUP_W), jnp.bfloat16),
            jax.ShapeDtypeStruct((b, d, sub_len, LANES), jnp.float32),
        ],
        scratch_shapes=[pltpu.VMEM((2, 2 * QBLK_A, kwin), jnp.float32),
                        pltpu.VMEM((2, 2 * QBLK_A, kwin), jnp.bfloat16),
                        pltpu.VMEM((2, QBLK_A, LANES), jnp.float32),
                        pltpu.VMEM((2, QBLK_A, LANES), jnp.float32)],
        compiler_params=pltpu.CompilerParams(
            dimension_semantics=("arbitrary", "arbitrary"), vmem_limit_bytes=VMEM_LIMIT),
        name=f"dilated_g{g}",
    )(seg, bias)


def _nbr_bias_table(rpb):
    rpb = rpb.astype(jnp.float32) * LOG2E
    n_dc = 2 * NA_KW - 1
    pad = GRID_W - NA_KW
    ext = jnp.concatenate([jnp.repeat(rpb[..., :1], pad, axis=-1), rpb,
                           jnp.repeat(rpb[..., n_dc - 1:], pad, axis=-1)], axis=-1)
    toep = jnp.stack([ext[..., GRID_W - 1 - q: 2 * GRID_W - 1 - q] for q in range(GRID_W)], axis=-2)
    qc = np.arange(GRID_W)[:, None]
    kc = np.arange(GRID_W)[None, :]
    cs = np.clip(qc - NA_KW // 2, 0, GRID_W - NA_KW)
    valid = (kc >= cs) & (kc < cs + NA_KW)
    toep = jnp.where(valid, toep, NEG_INF)
    tab = jnp.concatenate([toep[:, :NBR_BIAS_ROWS], toep[:, 1:]], axis=-1)
    return tab.reshape(PAIRS_B, 2, NBR_BIAS_ROWS, GRID_W, 2 * GRID_W)


def _nbr_kernel(q_ref, k_ref, v_ref, bias_ref, o_ref, s_buf, p_buf):
    lane = lax.broadcasted_iota(jnp.int32, (GRID_W, LANES), 1)

    def rows(r):
        if isinstance(r, int):
            rs = min(max(r - NA_KH // 2, 0), GRID_ROWS - NA_KH)
            return rs - r + NA_KH - 1, r * GRID_W, rs * GRID_W
        rs = jnp.clip(r - NA_KH // 2, 0, GRID_ROWS - NA_KH)
        return (rs - r + NA_KH - 1, pl.multiple_of(r * GRID_W, GRID_W), pl.multiple_of(rs * GRID_W, GRID_W))

    def scores(r, slot):
        _, q0, k0 = rows(r)
        s_buf[slot] = _score_dot(_stack_heads(q_ref[0, pl.ds(q0, GRID_W), :]), k_ref[0, pl.ds(k0, NBR_KEYS), :])

    def softmax(r, slot):
        var, _, _ = rows(r)

        def bias_rows(chunk):
            assert chunk.stop - chunk.start == GRID_W
            hh = chunk.start // GRID_W
            return jnp.concatenate([bias_ref[0, hh, var + i] for i in range(0, NA_KH, 2)], axis=1)

        _softmax_rows(s_buf.at[slot], bias_rows, p_buf.at[slot], 2 * GRID_W)

    def weighted_values(r, slot):
        _, q0, k0 = rows(r)
        pv, l = _values_with_ones(p_buf[slot], v_ref[0, pl.ds(k0, NBR_KEYS), :])
        pvn = pv * (1.0 / l)
        o_ref[0, pl.ds(q0, GRID_W), :] = jnp.where(lane < HEAD_DIM, pvn[:GRID_W], pvn[GRID_W:]).astype(o_ref.dtype)

    def row_group(j, first=False, last=False):
        for u in range(NBR_ROWS_PER_ITER):
            r, slot = j * NBR_ROWS_PER_ITER + u, u % 2
            if not (last and u == NBR_ROWS_PER_ITER - 1):
                scores(r + 1, 1 - slot)
            if not (first and u == 0):
                weighted_values(r - 1, 1 - slot)
            softmax(r, slot)

    n_iter = GRID_ROWS // NBR_ROWS_PER_ITER
    scores(0, 0)
    row_group(0, first=True)

    def body(j, carry):
        row_group(j)
        return carry

    if n_iter > 2:
        lax.fori_loop(1, n_iter - 1, body, 0)
    row_group(n_iter - 1, last=True)
    weighted_values(GRID_ROWS - 1, 1)


def _nbr_call(seg, bias):
    b, t, _ = seg.shape
    return pl.pallas_call(
        _nbr_kernel,
        grid=(PAIRS_B, b),
        in_specs=[
            pl.BlockSpec((1, t, PAIR_W), lambda p, i: (i, 0, p)),
            pl.BlockSpec((1, t, PAIR_W), lambda p, i: (i, 0, PAIRS_B + p)),
            pl.BlockSpec((1, t, PAIR_W), lambda p, i: (i, 0, 2 * PAIRS_B + p)),
            pl.BlockSpec((1, 2, NBR_BIAS_ROWS, GRID_W, 2 * GRID_W), lambda p, i: (p, 0, 0, 0, 0)),
        ],
        out_specs=pl.BlockSpec((1, t, PAIR_W), lambda p, i: (i, 0, p)),
        out_shape=jax.ShapeDtypeStruct((b, t, WIDTH_B), jnp.bfloat16),
        scratch_shapes=[pltpu.VMEM((2, 2 * GRID_W, NBR_KEYS), jnp.float32),
                        pltpu.VMEM((2, 2 * GRID_W, NBR_KEYS), jnp.bfloat16)],
        compiler_params=pltpu.CompilerParams(
            dimension_semantics=("arbitrary", "arbitrary"), vmem_limit_bytes=VMEM_LIMIT),
        name="nbr",
    )(seg, seg, seg, bias)


def _silu(z):
    return z * jax.nn.sigmoid(z)


def _merge_kernel(x_ref, o0_ref, o1_ref, o2_ref, st0_ref, st1_ref, st2_ref, z0_ref, z1_ref, z2_ref, ob_ref, zb_ref,
                  g_ref, expand_ref, wpa_ref, wpb_ref, wout_ref, bg_ref, npost_ref, y_ref, stats_nat, gated_nat):
    f32, bf16 = jnp.float32, jnp.bfloat16
    hm = x_ref.shape[1] // MERGE_SPLIT
    lane = lax.broadcasted_iota(jnp.int32, (hm, LANES), 1)
    n_slabs = GROUP_W // LANES

    for h in range(MERGE_SPLIT):
        rows = slice(h * hm, (h + 1) * hm)
        ub = (ob_ref[0, rows, :].astype(f32) * _silu(zb_ref[0, rows, :].astype(f32))).astype(bf16)
        yb = jnp.dot(ub, wpb_ref[...], preferred_element_type=f32)

        stats = [st0_ref[0, rows, :]]
        gated = [o0_ref[0, rows, :].astype(f32) * _silu(z0_ref[0, rows, :].astype(f32))]
        for gi, (d, o_ref, st_ref, z_ref) in enumerate(zip(DILATIONS[1:], (o1_ref, o2_ref), (st1_ref, st2_ref),
                                                             (z1_ref, z2_ref))):
            n = hm // d
            sub = slice(h * n, (h + 1) * n)
            for r in range(d):
                nat_rows = pl.ds(r, n, stride=d)
                stats_nat[h, gi, nat_rows, :] = st_ref[0, r, sub, :]
                w = o_ref[0, r, sub, :].astype(f32) * _silu(z_ref[0, r, sub, :].astype(f32))
                for c in range(n_slabs):
                    gated_nat[h, gi, c, nat_rows, :] = w[:, c * LANES:(c + 1) * LANES]
            stats.append(stats_nat[h, gi])
            gated.append(jnp.concatenate([gated_nat[h, gi, c] for c in range(n_slabs)], axis=1))
        sums = [pltpu.roll(st, LANES - HEADS_PER_GROUP_A, 1) for st in stats]
        mx = jnp.maximum(jnp.maximum(stats[0], stats[1]), stats[2])
        es = [jnp.exp2(st - mx) for st in stats]
        inv = 1.0 / (sums[0] * es[0] + sums[1] * es[1] + sums[2] * es[2])
        weights = [jnp.where(lane < HEADS_PER_GROUP_A, e * inv, 0.0) for e in es]
        comb = (weights[0] + pltpu.roll(weights[1], HEADS_PER_GROUP_A, 1)
                + pltpu.roll(weights[2], 2 * HEADS_PER_GROUP_A, 1))
        hi = comb.astype(bf16)
        lo = (comb - hi.astype(f32)).astype(bf16)
        wexp = jnp.dot(jnp.concatenate([hi, lo], axis=1), expand_ref[...], preferred_element_type=f32)
        ua = (jnp.concatenate(gated, axis=1) * wexp).astype(bf16)
        ya = jnp.dot(ua, wpa_ref[...], preferred_element_type=f32)
        ga = g_ref[0, rows, :D_MODEL].astype(f32) + bg_ref[0:1, :]
        gb = g_ref[0, rows, D_MODEL:].astype(f32) + bg_ref[1:2, :]
        merged = jax.nn.sigmoid(ga) * ya + jax.nn.sigmoid(gb) * yb
        out = jnp.dot(merged.astype(bf16), wout_ref[...], preferred_element_type=f32)
        ms = jnp.mean(out * out, axis=-1, keepdims=True)
        y_ref[0, rows, :] = x_ref[0, rows, :] + out * lax.rsqrt(ms + RMS_EPS) * npost_ref[...]


def _merge_call(x, segs_a, oas, stats, ob, seg_b, gates, b_gate, w_proj_a, w_proj_b, w_out, norm_post):
    b, t, _ = x.shape
    expand = np.zeros((2 * LANES, WIDTH_A), np.float32)
    for h in range(N_HEADS_A):
        expand[h, h * HEAD_DIM:(h + 1) * HEAD_DIM] = 1.0
        expand[LANES + h, h * HEAD_DIM:(h + 1) * HEAD_DIM] = 1.0
    z_col = QKV_A_W // GROUP_W

    def tok(w, col=0):
        return pl.BlockSpec((1, TM_OUT, w), lambda i, j: (i, j, col))

    def res_major(d, w, col=0):
        return pl.BlockSpec((1, d, TM_OUT // d, w), lambda i, j: (i, 0, j, col))

    full = lambda shp: pl.BlockSpec(shp, lambda i, j: (0,) * len(shp))
    d1, d2 = DILATIONS[1:]
    return pl.pallas_call(
        _merge_kernel,
        grid=(b, t // TM_OUT),
        in_specs=[tok(D_MODEL),
                  tok(GROUP_W), res_major(d1, GROUP_W), res_major(d2, GROUP_W),
                  tok(LANES), res_major(d1, LANES), res_major(d2, LANES),
                  tok(GROUP_W, z_col), res_major(d1, GROUP_W, z_col), res_major(d2, GROUP_W, z_col),
                  tok(WIDTH_B), tok(WIDTH_B, 3), tok(GATE_W),
                  full((2 * LANES, WIDTH_A)), full((WIDTH_A, D_MODEL)), full((WIDTH_B, D_MODEL)),
                  full((D_MODEL, D_MODEL)), full((2, D_MODEL)), full((1, D_MODEL))],
        out_specs=tok(D_MODEL),
        out_shape=jax.ShapeDtypeStruct((b, t, D_MODEL), x.dtype),
        scratch_shapes=[pltpu.VMEM((MERGE_SPLIT, 2, TM_OUT // MERGE_SPLIT, LANES), jnp.float32),
                        pltpu.VMEM((MERGE_SPLIT, 2, GROUP_W // LANES, TM_OUT // MERGE_SPLIT, LANES), jnp.float32)],
        compiler_params=pltpu.CompilerParams(
            dimension_semantics=("arbitrary", "arbitrary"), vmem_limit_bytes=VMEM_LIMIT),
        name="merge",
    )(x, *oas, *stats, *segs_a, ob, seg_b, gates, jnp.asarray(expand, jnp.bfloat16),
      w_proj_a.astype(jnp.bfloat16), w_proj_b.astype(jnp.bfloat16), w_out.astype(jnp.bfloat16),
      b_gate, norm_post.reshape(1, D_MODEL))


def _encoder_layer(x, norm_pre, w_perm, b_gate, nbr_bias, w_proj_a, w_proj_b, w_out, norm_post):
    b, t, _ = x.shape
    *segs_a, seg_b, gates = _inproj_call(x, norm_pre, w_perm)
    att = [_dilated_call(seg.reshape(b, DILATIONS[g], t // DILATIONS[g], SEG_A_W), g)
           for g, seg in enumerate(segs_a)]
    oas = [att[0][0].reshape(b, t, GROUP_W), att[1][0], att[2][0]]
    stats = [att[0][1].reshape(b, t, LANES), att[1][1], att[2][1]]
    ob = _nbr_call(seg_b, nbr_bias)
    return _merge_call(x, segs_a, oas, stats, ob, seg_b, gates, b_gate, w_proj_a, w_proj_b, w_out, norm_post)


def kernel(x_prompt, x_sample, norm_pre, w_in, b_gate, rpb, w_proj_a, w_proj_b, w_out, norm_post):
    y_prompt, y_sample = x_prompt, x_sample
    for l in range(norm_pre.shape[0]):
        w_perm = _permute_w_in(w_in, l)
        nbr_bias = _nbr_bias_table(rpb[l])
        args = (norm_pre[l], w_perm, b_gate[l], nbr_bias, w_proj_a[l], w_proj_b[l], w_out[l], norm_post[l])
        y_prompt = _encoder_layer(y_prompt, *args)
        y_sample = _encoder_layer(y_sample, *args)
    return (y_prompt, y_sample)
```

```python
import functools
import math

import numpy as np
import jax
import jax.numpy as jnp
from jax import lax
from jax.experimental import pallas as pl
from jax.experimental.pallas import tpu as pltpu

D_MODEL = 1024
SEQ = 2048
HEAD_DIM = 64
DIL_PATTERNS = ((128, 1), (512, 4), (2048, 16))
DILATIONS = tuple(d for _, d in DIL_PATTERNS)
N_GROUPS_A = 3
HEADS_PER_GROUP_A = 6
N_HEADS_A = N_GROUPS_A * HEADS_PER_GROUP_A
WIDTH_A = N_HEADS_A * HEAD_DIM
GROUP_W = HEADS_PER_GROUP_A * HEAD_DIM
N_HEADS_B = 14
WIDTH_B = N_HEADS_B * HEAD_DIM
GRID_W = 64
GRID_ROWS = SEQ // GRID_W
NA_KH = 8
NA_KW = 16
RMS_EPS = 1e-6
NEG_INF = -1e30
LOG2E = math.log2(math.e)

LANES = 128
N_SLABS = D_MODEL // LANES
PAIR_W = 2 * HEAD_DIM
PAIRS_A = HEADS_PER_GROUP_A // 2
PAIRS_B = N_HEADS_B // 2
HALF_WIN = 64
QBLK_A = 128
KWIN_A = QBLK_A + 2 * HALF_WIN
NBR_KEYS = NA_KH * GRID_W
NBR_BIAS_ROWS = 2 * NA_KH - 2
SOFTMAX_CHUNK = 32 * 1024
NBR_ROWS_PER_ITER = 16
DIL_UNITS_PER_STEP = 16
VMEM_LIMIT = 56 * 1024 * 1024

QKV_A_W = 3 * GROUP_W
SEG_A_W = 4 * GROUP_W
SEG_B_W = 4 * WIDTH_B
GATE_W = 2 * D_MODEL
INPROJ_SEGMENTS = tuple((d, SEG_A_W) for d in DILATIONS) + ((1, SEG_B_W), (1, GATE_W))
PROJ_CHUNK = 512
TM_IN = 512
TM_OUT = 512
MERGE_SPLIT = 1


def _permute_w_in(w_in, layer):
    scale = LOG2E * HEAD_DIM ** -0.5
    qa, ka, va, za = (w_in[layer, :, i * WIDTH_A:(i + 1) * WIDTH_A] for i in range(4))
    ob = 4 * WIDTH_A
    qb, kb, vb, zb = (w_in[layer, :, ob + i * WIDTH_B: ob + (i + 1) * WIDTH_B] for i in range(4))
    gates = w_in[layer, :, ob + 4 * WIDTH_B:]
    cols = []
    for g in range(N_GROUPS_A):
        sl = slice(g * GROUP_W, (g + 1) * GROUP_W)
        cols += [qa[:, sl] * scale, ka[:, sl], va[:, sl], za[:, sl]]
    cols += [qb * scale, kb, vb, zb, gates]
    return jnp.concatenate(cols, axis=1).astype(jnp.bfloat16)


def _inproj_kernel(*refs):
    n_seg = len(INPROJ_SEGMENTS)
    x_refs = refs[:N_SLABS]
    xf_ref, g_ref, w_ref = refs[N_SLABS:N_SLABS + 3]
    out_refs = refs[N_SLABS + 3:N_SLABS + 3 + n_seg]
    hn_refs = dict(zip(DILATIONS[1:], refs[N_SLABS + 3 + n_seg:]))

    def normalise(slabs, hn_ref, rows):
        ssq = sum(jnp.sum(s * s, axis=-1, keepdims=True) for s in slabs)
        rstd = lax.rsqrt(ssq * (1.0 / D_MODEL) + RMS_EPS)
        for c, s in enumerate(slabs):
            cols = slice(c * LANES, (c + 1) * LANES)
            hn_ref[rows, cols] = (s * rstd * g_ref[:, cols]).astype(hn_ref.dtype)

    xf = xf_ref[0]
    hn_nat = (xf * lax.rsqrt(jnp.mean(xf * xf, axis=-1, keepdims=True) + RMS_EPS) * g_ref[...]).astype(jnp.bfloat16)
    for d in DILATIONS[1:]:
        n = TM_IN // d
        for r in range(d):
            normalise([x_ref[0, pl.ds(r, n, stride=d), :] for x_ref in x_refs], hn_refs[d],
                      slice(r * n, (r + 1) * n))
    col = 0
    for o_ref, (d, width) in zip(out_refs, INPROJ_SEGMENTS):
        n = TM_IN // d
        for a in range(0, width, PROJ_CHUNK):
            res = jnp.dot(hn_nat if d == 1 else hn_refs[d][...], w_ref[:, col + a:col + a + PROJ_CHUNK],
                          preferred_element_type=jnp.float32).astype(jnp.bfloat16)
            if d == 1:
                o_ref[0, :, a:a + PROJ_CHUNK] = res
            else:
                for r in range(d):
                    o_ref[0, r, :, a:a + PROJ_CHUNK] = res[r * n:(r + 1) * n]
        col += width


def _inproj_call(x, norm_pre, w_perm):
    b, t, _ = x.shape
    bf16 = jnp.bfloat16
    out_shape, out_specs = [], []
    for d, width in INPROJ_SEGMENTS:
        if d == 1:
            out_shape.append(jax.ShapeDtypeStruct((b, t, width), bf16))
            out_specs.append(pl.BlockSpec((1, TM_IN, width), lambda i, j: (i, j, 0)))
        else:
            out_shape.append(jax.ShapeDtypeStruct((b, d, t // d, width), bf16))
            out_specs.append(pl.BlockSpec((1, d, TM_IN // d, width), lambda i, j: (i, 0, j, 0)))
    x_specs = [pl.BlockSpec((1, TM_IN, LANES), functools.partial(lambda i, j, c: (i, j, c), c=c))
               for c in range(N_SLABS)]
    return pl.pallas_call(
        _inproj_kernel,
        grid=(b, t // TM_IN),
        in_specs=x_specs + [
            pl.BlockSpec((1, TM_IN, D_MODEL), lambda i, j: (i, j, 0)),
            pl.BlockSpec((1, D_MODEL), lambda i, j: (0, 0)),
            pl.BlockSpec(w_perm.shape, lambda i, j: (0, 0), pipeline_mode=pl.Buffered(1)),
        ],
        out_specs=out_specs,
        out_shape=out_shape,
        scratch_shapes=[pltpu.VMEM((TM_IN, D_MODEL), bf16)] * (len(DILATIONS) - 1),
        compiler_params=pltpu.CompilerParams(
            dimension_semantics=("arbitrary", "arbitrary"), vmem_limit_bytes=VMEM_LIMIT),
        name="inproj",
    )(*([x] * N_SLABS), x, norm_pre.reshape(1, D_MODEL), w_perm)


def _stack_heads(q2):
    lane = lax.broadcasted_iota(jnp.int32, q2.shape, 1)
    zero = jnp.zeros_like(q2)
    return jnp.concatenate([jnp.where(lane < HEAD_DIM, q2, zero), jnp.where(lane < HEAD_DIM, zero, q2)], axis=0)


def _score_dot(qs, k2):
    return lax.dot_general(qs, k2, (((1,), (1,)), ((), ())), preferred_element_type=jnp.float32)


def _softmax_rows(s_ref, bias_rows, p_ref, n_rows):
    chunk = SOFTMAX_CHUNK // s_ref.shape[-1]
    maxima = []
    for r0 in range(0, n_rows, chunk):
        rows = slice(r0, r0 + chunk)
        s = s_ref[rows, :] + bias_rows(rows)
        m = jnp.max(s, axis=-1, keepdims=True)
        p_ref[rows, :] = jnp.exp2(s - m).astype(p_ref.dtype)
        maxima.append(jnp.broadcast_to(m, (chunk, LANES)))
    return jnp.concatenate(maxima, axis=0)


def _values_with_ones(p, v2):
    vx = jnp.concatenate([v2, jnp.ones_like(v2)], axis=1)
    pvx = jnp.dot(p, vx, preferred_element_type=jnp.float32)
    return pvx[:, :LANES], pvx[:, LANES:]


def _alibi_slopes():
    return (2.0 ** (-8.0 * np.arange(1, N_HEADS_A + 1) / N_HEADS_A)).astype(np.float32)


def _dilated_bias_table(g, seq_len):
    d = DILATIONS[g]
    sub_len = seq_len // d
    kwin = min(KWIN_A, sub_len)
    deltas = (-HALF_WIN, 0, QBLK_A - kwin) if sub_len > QBLK_A else (0,)
    slopes = _alibi_slopes()[g::N_GROUPS_A]
    i = np.arange(QBLK_A)[:, None]
    j = np.arange(kwin)[None, :]
    tab = np.empty((PAIRS_A, len(deltas), 2 * QBLK_A, kwin), np.float32)
    for vi, delta in enumerate(deltas):
        rel = np.abs(j - i + delta)
        for p in range(PAIRS_A):
            for hh in range(2):
                bias = np.where(rel <= HALF_WIN,
                                np.float32(LOG2E) * (-slopes[2 * p + hh] * (rel * d).astype(np.float32)), NEG_INF)
                tab[p, vi, hh * QBLK_A:(hh + 1) * QBLK_A] = bias
    return jnp.asarray(tab)


def _dilated_kernel(qkv_ref, bias_ref, o_ref, st_ref, s_buf, p_buf, m_scr, l_scr, *, sub_len, n_res):
    kwin = min(KWIN_A, sub_len)
    n_blk = sub_len // QBLK_A
    n_units = n_blk * n_res
    lane = lax.broadcasted_iota(jnp.int32, (QBLK_A, LANES), 1)

    def unit(j):
        res, n = divmod(j, n_blk)
        if n_blk == 1:
            return res, 0, 0, 0
        k0 = min(max(n * QBLK_A - HALF_WIN, 0), sub_len - kwin)
        return res, (1 if n == 0 else 2 if n == n_blk - 1 else 0), n * QBLK_A, k0

    def scores(j, p, slot):
        res, _, q0, k0 = unit(j)
        c = p * PAIR_W
        qs = _stack_heads(qkv_ref[0, res, pl.ds(q0, QBLK_A), c:c + PAIR_W])
        s_buf[slot] = _score_dot(qs, qkv_ref[0, res, pl.ds(k0, kwin), c + GROUP_W:c + GROUP_W + PAIR_W])

    def head_lanes(tile_ref, j, p, first_lane, col):
        tile = jnp.zeros((QBLK_A, LANES), jnp.float32) if p == 0 else tile_ref[j % 2]
        tile = jnp.where(lane == first_lane + 2 * p, col[:QBLK_A], tile)
        return jnp.where(lane == first_lane + 2 * p + 1, col[QBLK_A:], tile)

    def softmax(j, p, slot):
        _, var, _, _ = unit(j)
        m = _softmax_rows(s_buf.at[slot], lambda rows: bias_ref[p, var, rows, :], p_buf.at[slot], 2 * QBLK_A)
        m_scr[j % 2] = head_lanes(m_scr, j, p, 0, m)

    def weighted_values(j, p, slot):
        res, _, q0, k0 = unit(j)
        c = p * PAIR_W + 2 * GROUP_W
        pv, l = _values_with_ones(p_buf[slot], qkv_ref[0, res, pl.ds(k0, kwin), c:c + PAIR_W])
        o_ref[0, res, pl.ds(q0, QBLK_A), p * PAIR_W:(p + 1) * PAIR_W] = (
            jnp.where(lane < HEAD_DIM, pv[:QBLK_A], pv[QBLK_A:]).astype(o_ref.dtype))
        sums = head_lanes(l_scr, j, p, HEADS_PER_GROUP_A, l)
        if p == PAIRS_A - 1:
            st_ref[0, res, pl.ds(q0, QBLK_A), :] = jnp.where(lane < HEADS_PER_GROUP_A, m_scr[j % 2], sums)
        else:
            l_scr[j % 2] = sums

    items = [(j, p) for j in range(n_units) for p in range(PAIRS_A)]
    scores(*items[0], 0)
    for i, item in enumerate(items):
        slot = i % 2
        if i + 1 < len(items):
            scores(*items[i + 1], 1 - slot)
        if i > 0:
            weighted_values(*items[i - 1], 1 - slot)
        softmax(*item, slot)
    weighted_values(*items[-1], (len(items) - 1) % 2)


def _dilated_call(seg, g):
    b, d, sub_len, _ = seg.shape
    n_res = min(d, DIL_UNITS_PER_STEP * QBLK_A // sub_len)
    kwin = min(KWIN_A, sub_len)
    bias = _dilated_bias_table(g, d * sub_len)
    return pl.pallas_call(
        functools.partial(_dilated_kernel, sub_len=sub_len, n_res=n_res),
        grid=(b, d // n_res),
        in_specs=[
            pl.BlockSpec((1, n_res, sub_len, QKV_A_W), lambda i, j: (i, j, 0, 0)),
            pl.BlockSpec(bias.shape, lambda i, j: (0, 0, 0, 0)),
        ],
        out_specs=[
            pl.BlockSpec((1, n_res, sub_len, GROUP_W), lambda i, j: (i, j, 0, 0)),
            pl.BlockSpec((1, n_res, sub_len, LANES), lambda i, j: (i, j, 0, 0)),
        ],
        out_shape=[
            jax.ShapeDtypeStruct((b, d, sub_len, GROUP_W), jnp.bfloat16),
            jax.ShapeDtypeStruct((b, d, sub_len, LANES), jnp.float32),
        ],
        scratch_shapes=[pltpu.VMEM((2, 2 * QBLK_A, kwin), jnp.float32),
                        pltpu.VMEM((2, 2 * QBLK_A, kwin), jnp.bfloat16),
                        pltpu.VMEM((2, QBLK_A, LANES), jnp.float32),
                        pltpu.VMEM((2, QBLK_A, LANES), jnp.float32)],
        compiler_params=pltpu.CompilerParams(
            dimension_semantics=("arbitrary", "arbitrary"), vmem_limit_bytes=VMEM_LIMIT),
        name=f"dilated_g{g}",
    )(seg, bias)


def _nbr_bias_table(rpb):
    rpb = rpb.astype(jnp.float32) * LOG2E
    n_dc = 2 * NA_KW - 1
    pad = GRID_W - NA_KW
    ext = jnp.concatenate([jnp.repeat(rpb[..., :1], pad, axis=-1), rpb,
                           jnp.repeat(rpb[..., n_dc - 1:], pad + 1, axis=-1)], axis=-1)
    return ext.reshape(PAIRS_B, 2, 2 * NA_KH - 1, LANES)


def _nbr_kernel(q_ref, k_ref, v_ref, ext_ref, o_ref, s_buf, p_buf, bias_scr):
    lane = lax.broadcasted_iota(jnp.int32, (GRID_W, LANES), 1)

    @pl.when(pl.program_id(1) == 0)
    def _():
        qcol = lax.broadcasted_iota(jnp.int32, (GRID_W, LANES), 0)
        kcol = jnp.bitwise_and(lane, GRID_W - 1)
        first = jnp.clip(qcol - NA_KW // 2, 0, GRID_W - NA_KW)
        valid = jnp.logical_and(kcol >= first, kcol < first + NA_KW)
        for hh in range(2):
            for dr in range(NBR_BIAS_ROWS):
                left = pltpu.roll(jnp.broadcast_to(ext_ref[0, hh, dr:dr + 1, :], (GRID_W, LANES)),
                                  LANES - GRID_W + 1, 1, stride=1, stride_axis=0)
                right = pltpu.roll(jnp.broadcast_to(ext_ref[0, hh, dr + 1:dr + 2, :], (GRID_W, LANES)),
                                   1, 1, stride=1, stride_axis=0)
                bias_scr[hh, dr] = jnp.where(valid, jnp.where(lane < GRID_W, left, right), NEG_INF)

    def rows(r):
        if isinstance(r, int):
            rs = min(max(r - NA_KH // 2, 0), GRID_ROWS - NA_KH)
            return rs - r + NA_KH - 1, r * GRID_W, rs * GRID_W
        rs = jnp.clip(r - NA_KH // 2, 0, GRID_ROWS - NA_KH)
        return (rs - r + NA_KH - 1, pl.multiple_of(r * GRID_W, GRID_W), pl.multiple_of(rs * GRID_W, GRID_W))

    def scores(r, slot):
        _, q0, k0 = rows(r)
        s_buf[slot] = _score_dot(_stack_heads(q_ref[0, pl.ds(q0, GRID_W), :]), k_ref[0, pl.ds(k0, NBR_KEYS), :])

    def softmax(r, slot):
        var, _, _ = rows(r)

        def bias_rows(chunk):
            assert chunk.stop - chunk.start == GRID_W
            hh = chunk.start // GRID_W
            return jnp.concatenate([bias_scr[hh, var + i] for i in range(0, NA_KH, 2)], axis=1)

        _softmax_rows(s_buf.at[slot], bias_rows, p_buf.at[slot], 2 * GRID_W)

    def weighted_values(r, slot):
        _, q0, k0 = rows(r)
        pv, l = _values_with_ones(p_buf[slot], v_ref[0, pl.ds(k0, NBR_KEYS), :])
        pvn = pv * (1.0 / l)
        o_ref[0, pl.ds(q0, GRID_W), :] = jnp.where(lane < HEAD_DIM, pvn[:GRID_W], pvn[GRID_W:]).astype(o_ref.dtype)

    def row_group(j, first=False, last=False):
        for u in range(NBR_ROWS_PER_ITER):
            r, slot = j * NBR_ROWS_PER_ITER + u, u % 2
            if not (last and u == NBR_ROWS_PER_ITER - 1):
                scores(r + 1, 1 - slot)
            if not (first and u == 0):
                weighted_values(r - 1, 1 - slot)
            softmax(r, slot)

    n_iter = GRID_ROWS // NBR_ROWS_PER_ITER
    scores(0, 0)
    row_group(0, first=True)

    def body(j, carry):
        row_group(j)
        return carry

    if n_iter > 2:
        lax.fori_loop(1, n_iter - 1, body, 0)
    row_group(n_iter - 1, last=True)
    weighted_values(GRID_ROWS - 1, 1)


def _nbr_call(seg, bias):
    b, t, _ = seg.shape
    return pl.pallas_call(
        _nbr_kernel,
        grid=(PAIRS_B, b),
        in_specs=[
            pl.BlockSpec((1, t, PAIR_W), lambda p, i: (i, 0, p)),
            pl.BlockSpec((1, t, PAIR_W), lambda p, i: (i, 0, PAIRS_B + p)),
            pl.BlockSpec((1, t, PAIR_W), lambda p, i: (i, 0, 2 * PAIRS_B + p)),
            pl.BlockSpec((1, 2, 2 * NA_KH - 1, LANES), lambda p, i: (p, 0, 0, 0)),
        ],
        out_specs=pl.BlockSpec((1, t, PAIR_W), lambda p, i: (i, 0, p)),
        out_shape=jax.ShapeDtypeStruct((b, t, WIDTH_B), jnp.bfloat16),
        scratch_shapes=[pltpu.VMEM((2, 2 * GRID_W, NBR_KEYS), jnp.float32),
                        pltpu.VMEM((2, 2 * GRID_W, NBR_KEYS), jnp.bfloat16),
                        pltpu.VMEM((2, NBR_BIAS_ROWS, GRID_W, LANES), jnp.float32)],
        compiler_params=pltpu.CompilerParams(
            dimension_semantics=("arbitrary", "arbitrary"), vmem_limit_bytes=VMEM_LIMIT),
        name="nbr",
    )(seg, seg, seg, bias)


def _silu(z):
    return z * jax.nn.sigmoid(z)


def _merge_kernel(x_ref, o0_ref, o1_ref, o2_ref, st0_ref, st1_ref, st2_ref, z0_ref, z1_ref, z2_ref, ob_ref, zb_ref,
                  g_ref, expand_ref, wpa_ref, wpb_ref, wout_ref, bg_ref, npost_ref, y_ref, stats_nat, gated_nat):
    f32, bf16 = jnp.float32, jnp.bfloat16
    hm = x_ref.shape[1] // MERGE_SPLIT
    lane = lax.broadcasted_iota(jnp.int32, (hm, LANES), 1)
    n_slabs = GROUP_W // LANES

    for h in range(MERGE_SPLIT):
        rows = slice(h * hm, (h + 1) * hm)
        ub = (ob_ref[0, rows, :].astype(f32) * _silu(zb_ref[0, rows, :].astype(f32))).astype(bf16)
        yb = jnp.dot(ub, wpb_ref[...], preferred_element_type=f32)

        stats = [st0_ref[0, rows, :]]
        gated = [o0_ref[0, rows, :].astype(f32) * _silu(z0_ref[0, rows, :].astype(f32))]
        for gi, (d, o_ref, st_ref, z_ref) in enumerate(zip(DILATIONS[1:], (o1_ref, o2_ref), (st1_ref, st2_ref),
                                                             (z1_ref, z2_ref))):
            n = hm // d
            sub = slice(h * n, (h + 1) * n)
            for r in range(d):
                nat_rows = pl.ds(r, n, stride=d)
                stats_nat[h, gi, nat_rows, :] = st_ref[0, r, sub, :]
                w = o_ref[0, r, sub, :].astype(f32) * _silu(z_ref[0, r, sub, :].astype(f32))
                for c in range(n_slabs):
                    gated_nat[h, gi, c, nat_rows, :] = w[:, c * LANES:(c + 1) * LANES]
            stats.append(stats_nat[h, gi])
            gated.append(jnp.concatenate([gated_nat[h, gi, c] for c in range(n_slabs)], axis=1))
        sums = [pltpu.roll(st, LANES - HEADS_PER_GROUP_A, 1) for st in stats]
        mx = jnp.maximum(jnp.maximum(stats[0], stats[1]), stats[2])
        es = [jnp.exp2(st - mx) for st in stats]
        inv = 1.0 / (sums[0] * es[0] + sums[1] * es[1] + sums[2] * es[2])
        weights = [jnp.where(lane < HEADS_PER_GROUP_A, e * inv, 0.0) for e in es]
        comb = (weights[0] + pltpu.roll(weights[1], HEADS_PER_GROUP_A, 1)
                + pltpu.roll(weights[2], 2 * HEADS_PER_GROUP_A, 1))
        hi = comb.astype(bf16)
        lo = (comb - hi.astype(f32)).astype(bf16)
        wexp = jnp.dot(jnp.concatenate([hi, lo], axis=1), expand_ref[...], preferred_element_type=f32)
        ua = (jnp.concatenate(gated, axis=1) * wexp).astype(bf16)
        ya = jnp.dot(ua, wpa_ref[...], preferred_element_type=f32)
        ga = g_ref[0, rows, :D_MODEL].astype(f32) + bg_ref[0:1, :]
        gb = g_ref[0, rows, D_MODEL:].astype(f32) + bg_ref[1:2, :]
        merged = jax.nn.sigmoid(ga) * ya + jax.nn.sigmoid(gb) * yb
        out = jnp.dot(merged.astype(bf16), wout_ref[...], preferred_element_type=f32)
        ms = jnp.mean(out * out, axis=-1, keepdims=True)
        y_ref[0, rows, :] = x_ref[0, rows, :] + out * lax.rsqrt(ms + RMS_EPS) * npost_ref[...]


def _merge_call(x, segs_a, oas, stats, ob, seg_b, gates, b_gate, w_proj_a, w_proj_b, w_out, norm_post):
    b, t, _ = x.shape
    expand = np.zeros((2 * LANES, WIDTH_A), np.float32)
    for h in range(N_HEADS_A):
        expand[h, h * HEAD_DIM:(h + 1) * HEAD_DIM] = 1.0
        expand[LANES + h, h * HEAD_DIM:(h + 1) * HEAD_DIM] = 1.0
    z_col = QKV_A_W // GROUP_W

    def tok(w, col=0):
        return pl.BlockSpec((1, TM_OUT, w), lambda i, j: (i, j, col))

    def res_major(d, w, col=0):
        return pl.BlockSpec((1, d, TM_OUT // d, w), lambda i, j: (i, 0, j, col))

    full = lambda shp: pl.BlockSpec(shp, lambda i, j: (0,) * len(shp))
    d1, d2 = DILATIONS[1:]
    return pl.pallas_call(
        _merge_kernel,
        grid=(b, t // TM_OUT),
        in_specs=[tok(D_MODEL),
                  tok(GROUP_W), res_major(d1, GROUP_W), res_major(d2, GROUP_W),
                  tok(LANES), res_major(d1, LANES), res_major(d2, LANES),
                  tok(GROUP_W, z_col), res_major(d1, GROUP_W, z_col), res_major(d2, GROUP_W, z_col),
                  tok(WIDTH_B), tok(WIDTH_B, 3), tok(GATE_W),
                  full((2 * LANES, WIDTH_A)), full((WIDTH_A, D_MODEL)), full((WIDTH_B, D_MODEL)),
                  full((D_MODEL, D_MODEL)), full((2, D_MODEL)), full((1, D_MODEL))],
        out_specs=tok(D_MODEL),
        out_shape=jax.ShapeDtypeStruct((b, t, D_MODEL), x.dtype),
        scratch_shapes=[pltpu.VMEM((MERGE_SPLIT, 2, TM_OUT // MERGE_SPLIT, LANES), jnp.float32),
                        pltpu.VMEM((MERGE_SPLIT, 2, GROUP_W // LANES, TM_OUT // MERGE_SPLIT, LANES), jnp.float32)],
        compiler_params=pltpu.CompilerParams(
            dimension_semantics=("arbitrary", "arbitrary"), vmem_limit_bytes=VMEM_LIMIT),
        name="merge",
    )(x, *oas, *stats, *segs_a, ob, seg_b, gates, jnp.asarray(expand, jnp.bfloat16),
      w_proj_a.astype(jnp.bfloat16), w_proj_b.astype(jnp.bfloat16), w_out.astype(jnp.bfloat16),
      b_gate, norm_post.reshape(1, D_MODEL))


def _encoder_layer(x, norm_pre, w_perm, b_gate, nbr_bias, w_proj_a, w_proj_b, w_out, norm_post):
    b, t, _ = x.shape
    *segs_a, seg_b, gates = _inproj_call(x, norm_pre, w_perm)
    att = [_dilated_call(seg.reshape(b, DILATIONS[g], t // DILATIONS[g], SEG_A_W), g)
           for g, seg in enumerate(segs_a)]
    oas = [att[0][0].reshape(b, t, GROUP_W), att[1][0], att[2][0]]
    stats = [att[0][1].reshape(b, t, LANES), att[1][1], att[2][1]]
    ob = _nbr_call(seg_b, nbr_bias)
    return _merge_call(x, segs_a, oas, stats, ob, seg_b, gates, b_gate, w_proj_a, w_proj_b, w_out, norm_post)


def kernel(x_prompt, x_sample, norm_pre, w_in, b_gate, rpb, w_proj_a, w_proj_b, w_out, norm_post):
    y_prompt, y_sample = x_prompt, x_sample
    for l in range(norm_pre.shape[0]):
        w_perm = _permute_w_in(w_in, l)
        nbr_bias = _nbr_bias_table(rpb[l])
        args = (norm_pre[l], w_perm, b_gate[l], nbr_bias, w_proj_a[l], w_proj_b[l], w_out[l], norm_post[l])
        y_prompt = _encoder_layer(y_prompt, *args)
        y_sample = _encoder_layer(y_sample, *args)
    return (y_prompt, y_sample)
```

```python
import functools
import math

import numpy as np
import jax
import jax.numpy as jnp
from jax import lax
from jax.experimental import pallas as pl
from jax.experimental.pallas import tpu as pltpu

D_MODEL = 1024
SEQ = 2048
HEAD_DIM = 64
DIL_PATTERNS = ((128, 1), (512, 4), (2048, 16))
DILATIONS = tuple(d for _, d in DIL_PATTERNS)
N_GROUPS_A = 3
HEADS_PER_GROUP_A = 6
N_HEADS_A = N_GROUPS_A * HEADS_PER_GROUP_A
WIDTH_A = N_HEADS_A * HEAD_DIM
GROUP_W = HEADS_PER_GROUP_A * HEAD_DIM
N_HEADS_B = 14
WIDTH_B = N_HEADS_B * HEAD_DIM
GRID_W = 64
GRID_ROWS = SEQ // GRID_W
NA_KH = 8
NA_KW = 16
RMS_EPS = 1e-6
NEG_INF = -1e30
LOG2E = math.log2(math.e)

LANES = 128
N_SLABS = D_MODEL // LANES
PAIR_W = 2 * HEAD_DIM
PAIRS_A = HEADS_PER_GROUP_A // 2
PAIRS_B = N_HEADS_B // 2
HALF_WIN = 64
QBLK_A = 128
KWIN_A = QBLK_A + 2 * HALF_WIN
NBR_KEYS = NA_KH * GRID_W
NBR_BIAS_ROWS = 2 * NA_KH - 2
SOFTMAX_CHUNK = 32 * 1024
NBR_ROWS_PER_ITER = 16
DIL_UNITS_PER_STEP = 16
VMEM_LIMIT = 56 * 1024 * 1024

QKV_A_W = 3 * GROUP_W
SEG_A_W = 4 * GROUP_W
SEG_B_W = 4 * WIDTH_B
GATE_W = 2 * D_MODEL
INPROJ_SEGMENTS = tuple((d, SEG_A_W) for d in DILATIONS) + ((1, SEG_B_W), (1, GATE_W))
PROJ_CHUNK = 512
TM_IN = 512
TM_OUT = 512


def _permute_w_in(w_in, layer):
    scale = LOG2E * HEAD_DIM ** -0.5
    qa, ka, va, za = (w_in[layer, :, i * WIDTH_A:(i + 1) * WIDTH_A] for i in range(4))
    ob = 4 * WIDTH_A
    qb, kb, vb, zb = (w_in[layer, :, ob + i * WIDTH_B: ob + (i + 1) * WIDTH_B] for i in range(4))
    gates = w_in[layer, :, ob + 4 * WIDTH_B:]
    cols = []
    for g in range(N_GROUPS_A):
        sl = slice(g * GROUP_W, (g + 1) * GROUP_W)
        cols += [qa[:, sl] * scale, ka[:, sl], va[:, sl], za[:, sl]]
    cols += [qb * scale, kb, vb, zb, gates]
    return jnp.concatenate(cols, axis=1).astype(jnp.bfloat16)


def _inproj_kernel(*refs):
    n_seg = len(INPROJ_SEGMENTS)
    x_refs = refs[:N_SLABS]
    xf_ref, g_ref, w_ref = refs[N_SLABS:N_SLABS + 3]
    out_refs = refs[N_SLABS + 3:N_SLABS + 3 + n_seg]
    hn_refs = dict(zip(DILATIONS[1:], refs[N_SLABS + 3 + n_seg:]))

    def normalise(slabs, hn_ref, rows):
        ssq = sum(jnp.sum(s * s, axis=-1, keepdims=True) for s in slabs)
        rstd = lax.rsqrt(ssq * (1.0 / D_MODEL) + RMS_EPS)
        for c, s in enumerate(slabs):
            cols = slice(c * LANES, (c + 1) * LANES)
            hn_ref[rows, cols] = (s * rstd * g_ref[:, cols]).astype(hn_ref.dtype)

    xf = xf_ref[0]
    hn_nat = (xf * lax.rsqrt(jnp.mean(xf * xf, axis=-1, keepdims=True) + RMS_EPS) * g_ref[...]).astype(jnp.bfloat16)
    for d in DILATIONS[1:]:
        n = TM_IN // d
        for r in range(d):
            normalise([x_ref[0, pl.ds(r, n, stride=d), :] for x_ref in x_refs], hn_refs[d],
                      slice(r * n, (r + 1) * n))
    col = 0
    for o_ref, (d, width) in zip(out_refs, INPROJ_SEGMENTS):
        n = TM_IN // d
        for a in range(0, width, PROJ_CHUNK):
            res = jnp.dot(hn_nat if d == 1 else hn_refs[d][...], w_ref[:, col + a:col + a + PROJ_CHUNK],
                          preferred_element_type=jnp.float32).astype(jnp.bfloat16)
            if d == 1:
                o_ref[0, :, a:a + PROJ_CHUNK] = res
            else:
                for r in range(d):
                    o_ref[0, r, :, a:a + PROJ_CHUNK] = res[r * n:(r + 1) * n]
        col += width


def _inproj_call(x, norm_pre, w_perm):
    b, t, _ = x.shape
    bf16 = jnp.bfloat16
    out_shape, out_specs = [], []
    for d, width in INPROJ_SEGMENTS:
        if d == 1:
            out_shape.append(jax.ShapeDtypeStruct((b, t, width), bf16))
            out_specs.append(pl.BlockSpec((1, TM_IN, width), lambda i, j: (i, j, 0)))
        else:
            out_shape.append(jax.ShapeDtypeStruct((b, d, t // d, width), bf16))
            out_specs.append(pl.BlockSpec((1, d, TM_IN // d, width), lambda i, j: (i, 0, j, 0)))
    x_specs = [pl.BlockSpec((1, TM_IN, LANES), functools.partial(lambda i, j, c: (i, j, c), c=c))
               for c in range(N_SLABS)]
    return pl.pallas_call(
        _inproj_kernel,
        grid=(b, t // TM_IN),
        in_specs=x_specs + [
            pl.BlockSpec((1, TM_IN, D_MODEL), lambda i, j: (i, j, 0)),
            pl.BlockSpec((1, D_MODEL), lambda i, j: (0, 0)),
            pl.BlockSpec(w_perm.shape, lambda i, j: (0, 0), pipeline_mode=pl.Buffered(1)),
        ],
        out_specs=out_specs,
        out_shape=out_shape,
        scratch_shapes=[pltpu.VMEM((TM_IN, D_MODEL), bf16)] * (len(DILATIONS) - 1),
        compiler_params=pltpu.CompilerParams(
            dimension_semantics=("arbitrary", "arbitrary"), vmem_limit_bytes=VMEM_LIMIT),
        name="inproj",
    )(*([x] * N_SLABS), x, norm_pre.reshape(1, D_MODEL), w_perm)


def _stack_heads(q2):
    lane = lax.broadcasted_iota(jnp.int32, q2.shape, 1)
    zero = jnp.zeros_like(q2)
    return jnp.concatenate([jnp.where(lane < HEAD_DIM, q2, zero), jnp.where(lane < HEAD_DIM, zero, q2)], axis=0)


def _score_dot(qs, k2):
    return lax.dot_general(qs, k2, (((1,), (1,)), ((), ())), preferred_element_type=jnp.float32)


def _softmax_rows(s_ref, bias_rows, p_ref, n_rows):
    chunk = SOFTMAX_CHUNK // s_ref.shape[-1]
    maxima = []
    for r0 in range(0, n_rows, chunk):
        rows = slice(r0, r0 + chunk)
        s = s_ref[rows, :] + bias_rows(rows)
        m = jnp.max(s, axis=-1, keepdims=True)
        p_ref[rows, :] = jnp.exp2(s - m).astype(p_ref.dtype)
        maxima.append(jnp.broadcast_to(m, (chunk, LANES)))
    return jnp.concatenate(maxima, axis=0)


def _values_with_ones(p, v2):
    vx = jnp.concatenate([v2, jnp.ones_like(v2)], axis=1)
    pvx = jnp.dot(p, vx, preferred_element_type=jnp.float32)
    return pvx[:, :LANES], pvx[:, LANES:]


def _alibi_slopes():
    return (2.0 ** (-8.0 * np.arange(1, N_HEADS_A + 1) / N_HEADS_A)).astype(np.float32)


def _dilated_bias_table(g, seq_len):
    d = DILATIONS[g]
    sub_len = seq_len // d
    kwin = min(KWIN_A, sub_len)
    deltas = (-HALF_WIN, 0, QBLK_A - kwin) if sub_len > QBLK_A else (0,)
    slopes = _alibi_slopes()[g::N_GROUPS_A]
    i = np.arange(QBLK_A)[:, None]
    j = np.arange(kwin)[None, :]
    tab = np.empty((PAIRS_A, len(deltas), 2 * QBLK_A, kwin), np.float32)
    for vi, delta in enumerate(deltas):
        rel = np.abs(j - i + delta)
        for p in range(PAIRS_A):
            for hh in range(2):
                bias = np.where(rel <= HALF_WIN,
                                np.float32(LOG2E) * (-slopes[2 * p + hh] * (rel * d).astype(np.float32)), NEG_INF)
                tab[p, vi, hh * QBLK_A:(hh + 1) * QBLK_A] = bias
    return jnp.asarray(tab)


def _dilated_kernel(qkv_ref, bias_ref, o_ref, st_ref, s_buf, p_buf, m_scr, l_scr, *, sub_len, n_res):
    kwin = min(KWIN_A, sub_len)
    n_blk = sub_len // QBLK_A
    n_units = n_blk * n_res
    lane = lax.broadcasted_iota(jnp.int32, (QBLK_A, LANES), 1)

    def unit(j):
        res, n = divmod(j, n_blk)
        if n_blk == 1:
            return res, 0, 0, 0
        k0 = min(max(n * QBLK_A - HALF_WIN, 0), sub_len - kwin)
        return res, (1 if n == 0 else 2 if n == n_blk - 1 else 0), n * QBLK_A, k0

    def scores(j, p, slot):
        res, _, q0, k0 = unit(j)
        c = p * PAIR_W
        qs = _stack_heads(qkv_ref[0, res, pl.ds(q0, QBLK_A), c:c + PAIR_W])
        s_buf[slot] = _score_dot(qs, qkv_ref[0, res, pl.ds(k0, kwin), c + GROUP_W:c + GROUP_W + PAIR_W])

    def head_lanes(tile_ref, j, p, first_lane, col):
        tile = jnp.zeros((QBLK_A, LANES), jnp.float32) if p == 0 else tile_ref[j % 2]
        tile = jnp.where(lane == first_lane + 2 * p, col[:QBLK_A], tile)
        return jnp.where(lane == first_lane + 2 * p + 1, col[QBLK_A:], tile)

    def softmax(j, p, slot):
        _, var, _, _ = unit(j)
        m = _softmax_rows(s_buf.at[slot], lambda rows: bias_ref[p, var, rows, :], p_buf.at[slot], 2 * QBLK_A)
        m_scr[j % 2] = head_lanes(m_scr, j, p, 0, m)

    def weighted_values(j, p, slot):
        res, _, q0, k0 = unit(j)
        c = p * PAIR_W + 2 * GROUP_W
        pv, l = _values_with_ones(p_buf[slot], qkv_ref[0, res, pl.ds(k0, kwin), c:c + PAIR_W])
        o_ref[0, res, pl.ds(q0, QBLK_A), p * PAIR_W:(p + 1) * PAIR_W] = (
            jnp.where(lane < HEAD_DIM, pv[:QBLK_A], pv[QBLK_A:]).astype(o_ref.dtype))
        sums = head_lanes(l_scr, j, p, HEADS_PER_GROUP_A, l)
        if p == PAIRS_A - 1:
            st_ref[0, res, pl.ds(q0, QBLK_A), :] = jnp.where(lane < HEADS_PER_GROUP_A, m_scr[j % 2], sums)
        else:
            l_scr[j % 2] = sums

    items = [(j, p) for j in range(n_units) for p in range(PAIRS_A)]
    scores(*items[0], 0)
    for i, item in enumerate(items):
        slot = i % 2
        if i + 1 < len(items):
            scores(*items[i + 1], 1 - slot)
        if i > 0:
            weighted_values(*items[i - 1], 1 - slot)
        softmax(*item, slot)
    weighted_values(*items[-1], (len(items) - 1) % 2)


def _dilated_call(seg, g):
    b, d, sub_len, _ = seg.shape
    n_res = min(d, DIL_UNITS_PER_STEP * QBLK_A // sub_len)
    kwin = min(KWIN_A, sub_len)
    bias = _dilated_bias_table(g, d * sub_len)
    return pl.pallas_call(
        functools.partial(_dilated_kernel, sub_len=sub_len, n_res=n_res),
        grid=(b, d // n_res),
        in_specs=[
            pl.BlockSpec((1, n_res, sub_len, QKV_A_W), lambda i, j: (i, j, 0, 0)),
            pl.BlockSpec(bias.shape, lambda i, j: (0, 0, 0, 0)),
        ],
        out_specs=[
            pl.BlockSpec((1, n_res, sub_len, GROUP_W), lambda i, j: (i, j, 0, 0)),
            pl.BlockSpec((1, n_res, sub_len, LANES), lambda i, j: (i, j, 0, 0)),
        ],
        out_shape=[
            jax.ShapeDtypeStruct((b, d, sub_len, GROUP_W), jnp.bfloat16),
            jax.ShapeDtypeStruct((b, d, sub_len, LANES), jnp.float32),
        ],
        scratch_shapes=[pltpu.VMEM((2, 2 * QBLK_A, kwin), jnp.float32),
                        pltpu.VMEM((2, 2 * QBLK_A, kwin), jnp.bfloat16),
                        pltpu.VMEM((2, QBLK_A, LANES), jnp.float32),
                        pltpu.VMEM((2, QBLK_A, LANES), jnp.float32)],
        compiler_params=pltpu.CompilerParams(
            dimension_semantics=("arbitrary", "arbitrary"), vmem_limit_bytes=VMEM_LIMIT),
        name=f"dilated_g{g}",
    )(seg, bias)


def _nbr_bias_table(rpb):
    rpb = rpb.astype(jnp.float32) * LOG2E
    n_dc = 2 * NA_KW - 1
    pad = GRID_W - NA_KW
    ext = jnp.concatenate([jnp.repeat(rpb[..., :1], pad, axis=-1), rpb,
                           jnp.repeat(rpb[..., n_dc - 1:], pad + 1, axis=-1)], axis=-1)
    return ext.reshape(PAIRS_B, 2, 2 * NA_KH - 1, LANES)


def _nbr_kernel(q_ref, k_ref, v_ref, ext_ref, o_ref, s_buf, p_buf, bias_scr):
    lane = lax.broadcasted_iota(jnp.int32, (GRID_W, LANES), 1)

    @pl.when(pl.program_id(1) == 0)
    def _():
        qcol = lax.broadcasted_iota(jnp.int32, (GRID_W, LANES), 0)
        kcol = jnp.bitwise_and(lane, GRID_W - 1)
        first = jnp.clip(qcol - NA_KW // 2, 0, GRID_W - NA_KW)
        valid = jnp.logical_and(kcol >= first, kcol < first + NA_KW)
        for hh in range(2):
            for dr in range(NBR_BIAS_ROWS):
                left = pltpu.roll(jnp.broadcast_to(ext_ref[0, hh, dr:dr + 1, :], (GRID_W, LANES)),
                                  LANES - GRID_W + 1, 1, stride=1, stride_axis=0)
                right = pltpu.roll(jnp.broadcast_to(ext_ref[0, hh, dr + 1:dr + 2, :], (GRID_W, LANES)),
                                   1, 1, stride=1, stride_axis=0)
                bias_scr[hh, dr] = jnp.where(valid, jnp.where(lane < GRID_W, left, right), NEG_INF)

    def rows(r):
        if isinstance(r, int):
            rs = min(max(r - NA_KH // 2, 0), GRID_ROWS - NA_KH)
            return rs - r + NA_KH - 1, r * GRID_W, rs * GRID_W
        rs = jnp.clip(r - NA_KH // 2, 0, GRID_ROWS - NA_KH)
        return (rs - r + NA_KH - 1, pl.multiple_of(r * GRID_W, GRID_W), pl.multiple_of(rs * GRID_W, GRID_W))

    def scores(r, slot):
        _, q0, k0 = rows(r)
        s_buf[slot] = _score_dot(_stack_heads(q_ref[0, pl.ds(q0, GRID_W), :]), k_ref[0, pl.ds(k0, NBR_KEYS), :])

    def softmax(r, slot):
        var, _, _ = rows(r)

        def bias_rows(chunk):
            assert chunk.stop - chunk.start == GRID_W
            hh = chunk.start // GRID_W
            return jnp.concatenate([bias_scr[hh, var + i] for i in range(0, NA_KH, 2)], axis=1)

        _softmax_rows(s_buf.at[slot], bias_rows, p_buf.at[slot], 2 * GRID_W)

    def weighted_values(r, slot):
        _, q0, k0 = rows(r)
        pv, l = _values_with_ones(p_buf[slot], v_ref[0, pl.ds(k0, NBR_KEYS), :])
        pvn = pv * (1.0 / l)
        o_ref[0, pl.ds(q0, GRID_W), :] = jnp.where(lane < HEAD_DIM, pvn[:GRID_W], pvn[GRID_W:]).astype(o_ref.dtype)

    def row_group(j, first=False, last=False):
        for u in range(NBR_ROWS_PER_ITER):
            r, slot = j * NBR_ROWS_PER_ITER + u, u % 2
            if not (last and u == NBR_ROWS_PER_ITER - 1):
                scores(r + 1, 1 - slot)
            if not (first and u == 0):
                weighted_values(r - 1, 1 - slot)
            softmax(r, slot)

    n_iter = GRID_ROWS // NBR_ROWS_PER_ITER
    scores(0, 0)
    row_group(0, first=True)

    def body(j, carry):
        row_group(j)
        return carry

    if n_iter > 2:
        lax.fori_loop(1, n_iter - 1, body, 0)
    row_group(n_iter - 1, last=True)
    weighted_values(GRID_ROWS - 1, 1)


def _nbr_call(seg, bias):
    b, t, _ = seg.shape
    return pl.pallas_call(
        _nbr_kernel,
        grid=(PAIRS_B, b),
        in_specs=[
            pl.BlockSpec((1, t, PAIR_W), lambda p, i: (i, 0, p)),
            pl.BlockSpec((1, t, PAIR_W), lambda p, i: (i, 0, PAIRS_B + p)),
            pl.BlockSpec((1, t, PAIR_W), lambda p, i: (i, 0, 2 * PAIRS_B + p)),
            pl.BlockSpec((1, 2, 2 * NA_KH - 1, LANES), lambda p, i: (p, 0, 0, 0)),
        ],
        out_specs=pl.BlockSpec((1, t, PAIR_W), lambda p, i: (i, 0, p)),
        out_shape=jax.ShapeDtypeStruct((b, t, WIDTH_B), jnp.bfloat16),
        scratch_shapes=[pltpu.VMEM((2, 2 * GRID_W, NBR_KEYS), jnp.float32),
                        pltpu.VMEM((2, 2 * GRID_W, NBR_KEYS), jnp.bfloat16),
                        pltpu.VMEM((2, NBR_BIAS_ROWS, GRID_W, LANES), jnp.float32)],
        compiler_params=pltpu.CompilerParams(
            dimension_semantics=("arbitrary", "arbitrary"), vmem_limit_bytes=VMEM_LIMIT),
        name="nbr",
    )(seg, seg, seg, bias)


def _silu(z):
    return z * jax.nn.sigmoid(z)


def _merge_kernel(x_ref, o0_ref, o1_ref, o2_ref, st0_ref, st1_ref, st2_ref, z0_ref, z1_ref, z2_ref, ob_ref, zb_ref,
                  g_ref, expand_ref, wpa_ref, wpb_ref, wout_ref, bg_ref, npost_ref, y_ref, stats_nat, gated_nat):
    f32, bf16 = jnp.float32, jnp.bfloat16
    tm = x_ref.shape[1]
    lane = lax.broadcasted_iota(jnp.int32, (tm, LANES), 1)
    n_slabs = GROUP_W // LANES

    ub = (ob_ref[0].astype(f32) * _silu(zb_ref[0].astype(f32))).astype(bf16)
    yb = jnp.dot(ub, wpb_ref[...], preferred_element_type=f32)

    stats = [st0_ref[0]]
    gated = [o0_ref[0].astype(f32) * _silu(z0_ref[0].astype(f32))]
    for gi, (d, o_ref, st_ref, z_ref) in enumerate(zip(DILATIONS[1:], (o1_ref, o2_ref), (st1_ref, st2_ref),
                                                         (z1_ref, z2_ref))):
        n = tm // d
        for r in range(d):
            nat_rows = pl.ds(r, n, stride=d)
            stats_nat[gi, nat_rows, :] = st_ref[0, r]
            w = o_ref[0, r].astype(f32) * _silu(z_ref[0, r].astype(f32))
            for c in range(n_slabs):
                gated_nat[gi, c, nat_rows, :] = w[:, c * LANES:(c + 1) * LANES]
        stats.append(stats_nat[gi])
        gated.append(jnp.concatenate([gated_nat[gi, c] for c in range(n_slabs)], axis=1))
    sums = [pltpu.roll(st, LANES - HEADS_PER_GROUP_A, 1) for st in stats]
    mx = jnp.maximum(jnp.maximum(stats[0], stats[1]), stats[2])
    es = [jnp.exp2(st - mx) for st in stats]
    inv = 1.0 / (sums[0] * es[0] + sums[1] * es[1] + sums[2] * es[2])
    weights = [jnp.where(lane < HEADS_PER_GROUP_A, e * inv, 0.0) for e in es]
    comb = (weights[0] + pltpu.roll(weights[1], HEADS_PER_GROUP_A, 1)
            + pltpu.roll(weights[2], 2 * HEADS_PER_GROUP_A, 1))
    hi = comb.astype(bf16)
    lo = (comb - hi.astype(f32)).astype(bf16)
    wexp = jnp.dot(jnp.concatenate([hi, lo], axis=1), expand_ref[...], preferred_element_type=f32)
    ua = (jnp.concatenate(gated, axis=1) * wexp).astype(bf16)
    ya = jnp.dot(ua, wpa_ref[...], preferred_element_type=f32)
    ga = g_ref[0, :, :D_MODEL].astype(f32) + bg_ref[0:1, :]
    gb = g_ref[0, :, D_MODEL:].astype(f32) + bg_ref[1:2, :]
    merged = jax.nn.sigmoid(ga) * ya + jax.nn.sigmoid(gb) * yb
    out = jnp.dot(merged.astype(bf16), wout_ref[...], preferred_element_type=f32)
    ms = jnp.mean(out * out, axis=-1, keepdims=True)
    y_ref[0] = x_ref[0] + out * lax.rsqrt(ms + RMS_EPS) * npost_ref[...]


def _merge_call(x, segs_a, oas, stats, ob, seg_b, gates, b_gate, w_proj_a, w_proj_b, w_out, norm_post):
    b, t, _ = x.shape
    expand = np.zeros((2 * LANES, WIDTH_A), np.float32)
    for h in range(N_HEADS_A):
        expand[h, h * HEAD_DIM:(h + 1) * HEAD_DIM] = 1.0
        expand[LANES + h, h * HEAD_DIM:(h + 1) * HEAD_DIM] = 1.0
    z_col = QKV_A_W // GROUP_W

    def tok(w, col=0):
        return pl.BlockSpec((1, TM_OUT, w), lambda i, j: (i, j, col))

    def res_major(d, w, col=0):
        return pl.BlockSpec((1, d, TM_OUT // d, w), lambda i, j: (i, 0, j, col))

    full = lambda shp: pl.BlockSpec(shp, lambda i, j: (0,) * len(shp))
    d1, d2 = DILATIONS[1:]
    return pl.pallas_call(
        _merge_kernel,
        grid=(b, t // TM_OUT),
        in_specs=[tok(D_MODEL),
                  tok(GROUP_W), res_major(d1, GROUP_W), res_major(d2, GROUP_W),
                  tok(LANES), res_major(d1, LANES), res_major(d2, LANES),
                  tok(GROUP_W, z_col), res_major(d1, GROUP_W, z_col), res_major(d2, GROUP_W, z_col),
                  tok(WIDTH_B), tok(WIDTH_B, 3), tok(GATE_W),
                  full((2 * LANES, WIDTH_A)), full((WIDTH_A, D_MODEL)), full((WIDTH_B, D_MODEL)),
                  full((D_MODEL, D_MODEL)), full((2, D_MODEL)), full((1, D_MODEL))],
        out_specs=tok(D_MODEL),
        out_shape=jax.ShapeDtypeStruct((b, t, D_MODEL), x.dtype),
        scratch_shapes=[pltpu.VMEM((2, TM_OUT, LANES), jnp.float32),
                        pltpu.VMEM((2, GROUP_W // LANES, TM_OUT, LANES), jnp.float32)],
        compiler_params=pltpu.CompilerParams(
            dimension_semantics=("arbitrary", "arbitrary"), vmem_limit_bytes=VMEM_LIMIT),
        name="merge",
    )(x, *oas, *stats, *segs_a, ob, seg_b, gates, jnp.asarray(expand, jnp.bfloat16),
      w_proj_a.astype(jnp.bfloat16), w_proj_b.astype(jnp.bfloat16), w_out.astype(jnp.bfloat16),
      b_gate, norm_post.reshape(1, D_MODEL))


def _encoder_layer(x, norm_pre, w_perm, b_gate, nbr_bias, w_proj_a, w_proj_b, w_out, norm_post):
    b, t, _ = x.shape
    *segs_a, seg_b, gates = _inproj_call(x, norm_pre, w_perm)
    att = [_dilated_call(seg.reshape(b, DILATIONS[g], t // DILATIONS[g], SEG_A_W), g)
           for g, seg in enumerate(segs_a)]
    oas = [att[0][0].reshape(b, t, GROUP_W), att[1][0], att[2][0]]
    stats = [att[0][1].reshape(b, t, LANES), att[1][1], att[2][1]]
    ob = _nbr_call(seg_b, nbr_bias)
    return _merge_call(x, segs_a, oas, stats, ob, seg_b, gates, b_gate, w_proj_a, w_proj_b, w_out, norm_post)


def kernel(x_prompt, x_sample, norm_pre, w_in, b_gate, rpb, w_proj_a, w_proj_b, w_out, norm_post):
    y_prompt, y_sample = x_prompt, x_sample
    for l in range(norm_pre.shape[0]):
        w_perm = _permute_w_in(w_in, l)
        nbr_bias = _nbr_bias_table(rpb[l])
        args = (norm_pre[l], w_perm, b_gate[l], nbr_bias, w_proj_a[l], w_proj_b[l], w_out[l], norm_post[l])
        y_prompt = _encoder_layer(y_prompt, *args)
        y_sample = _encoder_layer(y_sample, *args)
    return (y_prompt, y_sample)
```
